```python
import jax, jax.numpy as jnp
from jax import lax
import numpy as np


D_MODEL = 1024
BATCH = 8
SEQ = 4096
DEPTH = 2

N_A_LAYERS = DEPTH // 2
N_B_LAYERS = DEPTH - N_A_LAYERS
HG_EXPAND = 128
HG_HEADS = D_MODEL // HG_EXPAND
HG_DV = D_MODEL // HG_HEADS
HG_CHUNK = 32
ATT_HEAD_DIM = 64
ATT_Q_HEADS = D_MODEL // ATT_HEAD_DIM
ATT_KV_HEADS = 2
ATT_GROUP = ATT_Q_HEADS // ATT_KV_HEADS
WINDOW = 128
D_FF = 2816
CONV_WIDTH = 3
EPS = 1e-6

kernel_name = 'yoco_hgrn2_swa_sink_alibi_convffn'

F32 = jnp.float32


def rms_norm(x, g):
    xf = x.astype(F32)
    xf = xf * lax.rsqrt(jnp.mean(xf * xf, axis=-1, keepdims=True) + EPS)
    return (xf * g.astype(F32)).astype(x.dtype)


def alibi_slopes(n_heads):
    return jnp.asarray(2.0 ** (-8.0 * np.arange(1, n_heads + 1) / n_heads), F32)


def hgrn2_chunked(q, k, v, logf):
    b_, s_, h_, dk = q.shape
    dv = v.shape[-1]
    n_chunks = s_ // HG_CHUNK

    def to_chunks(t):
        return t.reshape(b_, n_chunks, HG_CHUNK, h_, t.shape[-1]).transpose(1, 0, 3, 2, 4)

    qc, kc, vc, gc = to_chunks(q), to_chunks(k), to_chunks(v), to_chunks(logf)
    causal = jnp.tril(jnp.ones((HG_CHUNK, HG_CHUNK), bool))[:, :, None]

    def step(state, inp):
        qb, kb, vb, gb = inp
        cum = jnp.cumsum(gb, axis=2)
        o_inter = jnp.einsum('bhtk,bhkv->bhtv', qb * jnp.exp(cum), state)
        rel = cum[:, :, :, None, :] - cum[:, :, None, :, :]
        decay = jnp.exp(jnp.where(causal, rel, -jnp.inf))
        scores = jnp.einsum('bhtk,bhsk,bhtsk->bhts', qb, kb, decay)
        o_intra = jnp.einsum('bhts,bhsv->bhtv', scores, vb)
        last = cum[:, :, -1:, :]
        new_state = (jnp.exp(last[:, :, 0, :])[..., None] * state
                     + jnp.einsum('bhsk,bhsv->bhkv', kb * jnp.exp(last - cum), vb))
        return new_state, o_inter + o_intra

    s0 = jnp.zeros((b_, h_, dk, dv), F32)
    _, o = lax.scan(step, s0, (qc, kc, vc, gc))
    return o.transpose(1, 0, 3, 2, 4).reshape(b_, s_, h_, dv)


def hgrn2_mixer(x, w_in, lower_bound, out_norm, w_out):
    b_, s_, _ = x.shape
    q, f, i, g = jnp.split(x @ w_in, 4, axis=-1)
    q = jax.nn.silu(q.astype(F32)) * HG_EXPAND ** -0.5
    forget = lower_bound + (1.0 - lower_bound) * jax.nn.sigmoid(f.astype(F32))
    logf = jnp.log(forget)
    k = 1.0 - forget
    heads = lambda t: t.reshape(b_, s_, HG_HEADS, -1)
    o = hgrn2_chunked(heads(q), heads(k), heads(i.astype(F32)), heads(logf))
    o = rms_norm(o, out_norm) * jax.nn.silu(heads(g.astype(F32)))
    return o.reshape(b_, s_, D_MODEL).astype(x.dtype) @ w_out


def shared_kv(h, kv_norm, w_kv):
    b_, s_, _ = h.shape
    k, v = jnp.split(rms_norm(h, kv_norm) @ w_kv, 2, axis=-1)
    return (k.reshape(b_, s_, ATT_KV_HEADS, ATT_HEAD_DIM),
            v.reshape(b_, s_, ATT_KV_HEADS, ATT_HEAD_DIM))


def swa_sink_attention(x, k, v, w_q, sinks, w_o):
    b_, s_, _ = x.shape
    nb = s_ // WINDOW
    q = (x @ w_q).reshape(b_, nb, WINDOW, ATT_KV_HEADS, ATT_GROUP, ATT_HEAD_DIM)

    def band(t):
        tb = t.reshape(b_, nb, WINDOW, ATT_KV_HEADS, ATT_HEAD_DIM)
        prev = jnp.pad(tb[:, :-1], ((0, 0), (1, 0), (0, 0), (0, 0), (0, 0)))
        return jnp.concatenate([prev, tb], axis=2)

    kb, vb = band(k), band(v)
    scores = jnp.einsum('bnqkgd,bnskd->bnkgqs', q.astype(F32), kb.astype(F32)) * ATT_HEAD_DIM ** -0.5
    q_idx = jnp.arange(WINDOW)[:, None] + WINDOW
    k_idx = jnp.arange(2 * WINDOW)[None, :]
    dist = q_idx - k_idx
    key_abs = (jnp.arange(nb) * WINDOW)[:, None] + jnp.arange(2 * WINDOW)[None, :] - WINDOW
    valid = ((dist >= 0) & (dist < WINDOW))[None] & (key_abs >= 0)[:, None, :]
    slopes = alibi_slopes(ATT_Q_HEADS).reshape(ATT_KV_HEADS, ATT_GROUP)
    scores = scores - slopes[:, :, None, None] * dist.astype(F32)
    scores = jnp.where(valid[None, :, None, None], scores, -jnp.inf)
    sink = sinks.astype(F32).reshape(ATT_KV_HEADS, ATT_GROUP)[None, None, :, :, None, None]
    m = jnp.maximum(jnp.max(scores, axis=-1, keepdims=True), sink)
    e = jnp.exp(scores - m)
    probs = e / (jnp.sum(e, axis=-1, keepdims=True) + jnp.exp(sink - m))
    out = jnp.einsum('bnkgqs,bnskd->bnqkgd', probs, vb.astype(F32))
    return out.reshape(b_, s_, ATT_Q_HEADS * ATT_HEAD_DIM).astype(x.dtype) @ w_o


def conv_ffn(x, w_up, conv_w, conv_b, w_down):
    s_ = x.shape[1]
    gate, val = jnp.split(x @ w_up, 2, axis=-1)
    gp = jnp.pad(gate, ((0, 0), (CONV_WIDTH - 1, 0), (0, 0)))
    conv = conv_b
    for j in range(CONV_WIDTH):
        conv = conv + conv_w[j] * gp[:, j:j + s_]
    return (jax.nn.silu(conv) * val) @ w_down


def setup_inputs(seed: int = 0) -> dict:
    key = jax.random.key(seed)
    ks = jax.random.split(key, 18)
    D = D_MODEL
    HQD = ATT_Q_HEADS * ATT_HEAD_DIM
    KVD = ATT_KV_HEADS * ATT_HEAD_DIM

    def w(k, shape, fan_in):
        return jax.random.normal(k, shape, F32) * fan_in ** -0.5

    def gain(k, shape):
        return 1.0 + 0.02 * jax.random.normal(k, shape, F32)

    return {
        'x': jax.random.normal(ks[0], (BATCH, SEQ, D), F32),
        'hg_norm': gain(ks[1], (N_A_LAYERS, D)),
        'hg_w_in': w(ks[2], (N_A_LAYERS, D, 4 * D), D),
        'hg_lb_logits': 0.1 * jax.random.normal(ks[3], (N_A_LAYERS + 1, D), F32),
        'hg_out_norm': gain(ks[4], (N_A_LAYERS, HG_DV)),
        'hg_w_out': w(ks[5], (N_A_LAYERS, D, D), D),
        'kv_norm': gain(ks[6], (D,)),
        'w_kv': w(ks[7], (D, 2 * KVD), D),
        'attn_norm': gain(ks[8], (N_B_LAYERS, D)),
        'attn_w_q': w(ks[9], (N_B_LAYERS, D, HQD), D),
        'attn_sinks': 0.5 * jax.random.normal(ks[10], (N_B_LAYERS, ATT_Q_HEADS), F32),
        'attn_w_o': w(ks[11], (N_B_LAYERS, HQD, D), HQD),
        'ffn_norm': gain(ks[12], (DEPTH, D)),
        'ffn_w_up': w(ks[13], (DEPTH, D, 2 * D_FF), D),
        'ffn_conv_w': w(ks[14], (DEPTH, CONV_WIDTH, D_FF), CONV_WIDTH),
        'ffn_conv_b': 0.02 * jax.random.normal(ks[15], (DEPTH, D_FF), F32),
        'ffn_w_down': w(ks[16], (DEPTH, D_FF, D), D_FF),
        'final_norm': gain(ks[17], (D,)),
    }


def reference(x, hg_norm, hg_w_in, hg_lb_logits, hg_out_norm, hg_w_out, kv_norm, w_kv,
              attn_norm, attn_w_q, attn_sinks, attn_w_o, ffn_norm, ffn_w_up, ffn_conv_w,
              ffn_conv_b, ffn_w_down, final_norm):
    lower_bounds = jnp.cumsum(jax.nn.softmax(hg_lb_logits.astype(F32), axis=0), axis=0)
    h = x
    k_sh, v_sh = None, None
    for layer in range(DEPTH):
        if layer < N_A_LAYERS:
            a = layer
            h = h + hgrn2_mixer(rms_norm(h, hg_norm[a]), hg_w_in[a], lower_bounds[a],
                                hg_out_norm[a], hg_w_out[a])
        else:
            bi = layer - N_A_LAYERS
            if bi == 0:
                k_sh, v_sh = shared_kv(h, kv_norm, w_kv)
            h = h + swa_sink_attention(rms_norm(h, attn_norm[bi]), k_sh, v_sh,
                                       attn_w_q[bi], attn_sinks[bi], attn_w_o[bi])
        h = h + conv_ffn(rms_norm(h, ffn_norm[layer]), ffn_w_up[layer], ffn_conv_w[layer],
                         ffn_conv_b[layer], ffn_w_down[layer])
    return rms_norm(h, final_norm)
```

```python
import functools

import jax
import jax.numpy as jnp
from jax import lax
from jax.experimental import pallas as pl
from jax.experimental.pallas import tpu as pltpu

F32 = jnp.float32
BF16 = jnp.bfloat16
EPS = 1e-6

HG_HEADS = 8
HG_DK = 128
ATT_HEAD_DIM = 64
ATT_Q_HEADS = 16
ATT_KV_HEADS = 2
ATT_GROUP = ATT_Q_HEADS // ATT_KV_HEADS
WINDOW = 128
CONV_WIDTH = 3

HG_TILE = 256
HG_CHUNK = 64
FFN_TILE = 512
FFN_FCHUNK = 256
ATT_TILE = 512
SAFE_LOG_DECAY = -80.0
VMEM_LIMIT = 56 * 1024 * 1024
SUBLANES = 8
LANES = 128


def _dot(a, b):
    return jnp.dot(a, b, preferred_element_type=F32)


def _dot_nt(a, b):
    return lax.dot_general(a, b, (((1,), (1,)), ((), ())), preferred_element_type=F32)


def _rms(x, g):
    ms = jnp.mean(x * x, axis=-1, keepdims=True)
    return x * lax.rsqrt(ms + EPS) * g


def _sigmoid(x):
    return 1.0 / (1.0 + jnp.exp(-x))


def _resident(shape):
    nd = len(shape)
    return pl.BlockSpec(shape, lambda *_: (0,) * nd, pipeline_mode=pl.Buffered(1))


def _hgrn_kernel(x_ref, g_ref, lbl_ref, win_ref, onorm_ref, wout_ref, out_ref,
                 st_ref, q_s, k_s, v_s, cum_s, oi_s, o_s, *, layer):
    T, D = q_s.shape
    C = HG_CHUNK
    n_chunks = T // C
    s_idx = pl.program_id(1)

    @pl.when(s_idx == 0)
    def _():
        st_ref[...] = jnp.zeros_like(st_ref)

    x = x_ref[0]
    xn = _rms(x, g_ref[...]).astype(BF16)

    lg = lbl_ref[...]
    le = jnp.exp(lg - jnp.max(lg, axis=0, keepdims=True))
    lb = jnp.sum(le[0:layer + 1], axis=0, keepdims=True) / jnp.sum(le, axis=0, keepdims=True)

    pq = _dot(xn, win_ref[:, 0:D])
    q_s[...] = pq * _sigmoid(pq) * (HG_DK ** -0.5)
    pf = _dot(xn, win_ref[:, D:2 * D])
    forget = lb + (1.0 - lb) * _sigmoid(pf)
    k_s[...] = 1.0 - forget
    logf = jnp.log(forget)
    v_s[...] = _dot(xn, win_ref[:, 2 * D:3 * D])
    pg = _dot(xn, win_ref[:, 3 * D:4 * D])
    gate = pg * _sigmoid(pg)

    row = lax.broadcasted_iota(jnp.int32, (C, C), 0)
    col = lax.broadcasted_iota(jnp.int32, (C, C), 1)
    causal = row >= col
    tri = jnp.where(causal, 1.0, 0.0).astype(BF16)
    l_hi = logf.astype(BF16)
    l_lo = (logf - l_hi.astype(F32)).astype(BF16)
    for c in range(n_chunks):
        r = slice(c * C, (c + 1) * C)
        cum_s[r, :] = _dot(tri, l_hi[r]) + _dot(tri, l_lo[r])

    safe = jnp.min(cum_s[...]) > SAFE_LOG_DECAY

    @pl.when(safe)
    def _():
        for c in range(n_chunks):
            r = slice(c * C, (c + 1) * C)
            for h in range(HG_HEADS):
                hs = slice(h * HG_DK, (h + 1) * HG_DK)
                cum = cum_s[r, hs]
                qt = (q_s[r, hs] * jnp.exp(cum)).astype(BF16)
                kt = (k_s[r, hs] * jnp.exp(-cum)).astype(BF16)
                a = jnp.where(causal, _dot_nt(qt, kt), 0.0).astype(BF16)
                oi_s[r, hs] = _dot(a, v_s[r, hs].astype(BF16))

    @pl.when(jnp.logical_not(safe))
    def _():
        tcol = lax.broadcasted_iota(jnp.int32, (C, 1), 0)
        for c in range(n_chunks):
            r = slice(c * C, (c + 1) * C)
            qc = q_s[r, :]
            cumc = cum_s[r, :]

            def body(j, acc):
                kj = k_s[pl.ds(c * C + j, 1), :]
                vj = v_s[pl.ds(c * C + j, 1), :]
                cj = cum_s[pl.ds(c * C + j, 1), :]
                w = qc * kj * jnp.exp(jnp.minimum(cumc - cj, 0.0))
                keep = tcol >= j
                parts = []
                for h in range(HG_HEADS):
                    hs = slice(h * HG_DK, (h + 1) * HG_DK)
                    sc = jnp.sum(w[:, hs], axis=-1, keepdims=True)
                    parts.append(jnp.where(keep, sc, 0.0) * vj[:, hs])
                return acc + jnp.concatenate(parts, axis=-1)

            oi_s[r, :] = lax.fori_loop(0, C, body, jnp.zeros((C, D), F32))

    for c in range(n_chunks):
        r = slice(c * C, (c + 1) * C)
        for h in range(HG_HEADS):
            hs = slice(h * HG_DK, (h + 1) * HG_DK)
            cum = cum_s[r, hs]
            last = cum[C - 1:C, :]
            st = st_ref[h]
            qt = (q_s[r, hs] * jnp.exp(cum)).astype(BF16)
            o_s[r, hs] = oi_s[r, hs] + _dot_nt(qt, st.astype(BF16))
            kd = (k_s[r, hs] * jnp.exp(last - cum)).astype(BF16)
            vt = v_s[r, hs].T.astype(BF16)
            st_ref[h] = st * jnp.exp(last) + _dot(vt, kd)

    onorm = onorm_ref[...]
    parts = []
    for h in range(HG_HEADS):
        hs = slice(h * HG_DK, (h + 1) * HG_DK)
        parts.append(_rms(o_s[:, hs], onorm) * gate[:, hs])
    o = jnp.concatenate(parts, axis=-1).astype(BF16)
    out_ref[0] = x + _dot(o, wout_ref[...])


def _hgrn_layer(x, norm_g, lb_logits, w_in, out_norm, w_out, layer):
    B, S, D = x.shape
    T = min(HG_TILE, S)
    assert S % T == 0 and T % HG_CHUNK == 0 and D == HG_HEADS * HG_DK
    tile = pl.BlockSpec((1, T, D), lambda b, s: (b, s, 0))
    return pl.pallas_call(
        functools.partial(_hgrn_kernel, layer=layer),
        grid=(B, S // T),
        in_specs=[tile, _resident((1, D)), _resident(lb_logits.shape), _resident(w_in.shape),
                  _resident((1, HG_DK)), _resident(w_out.shape)],
        out_specs=tile,
        out_shape=jax.ShapeDtypeStruct(x.shape, F32),
        scratch_shapes=[pltpu.VMEM((HG_HEADS, HG_DK, HG_DK), F32)] + [pltpu.VMEM((T, D), F32)] * 6,
        compiler_params=pltpu.CompilerParams(dimension_semantics=("arbitrary", "arbitrary"),
                                             vmem_limit_bytes=VMEM_LIMIT),
        name="hgrn2_layer",
    )(x, norm_g.reshape(1, D), lb_logits, w_in, out_norm.reshape(1, HG_DK), w_out)


def _ffn_kernel(x_ref, g_ref, wup_ref, cw_ref, cb_ref, wdn_ref, fg_ref, out_ref, carry_ref, act_ref,
                *, final_norm):
    T, F = act_ref.shape
    FC = FFN_FCHUNK
    s_idx = pl.program_id(1)
    x = x_ref[0]
    xn = _rms(x, g_ref[...]).astype(BF16)
    has_prev = s_idx > 0
    rid = lax.broadcasted_iota(jnp.int32, (SUBLANES, 1), 0)
    for j in range(F // FC):
        cs = slice(j * FC, (j + 1) * FC)
        gate = _dot(xn, wup_ref[:, cs])
        val = _dot(xn, wup_ref[:, F + j * FC:F + (j + 1) * FC])
        prev = jnp.where(has_prev, carry_ref[:, cs], 0.0)
        carry_ref[:, cs] = gate[T - SUBLANES:, :]
        p1 = prev[SUBLANES - 1:SUBLANES, :]
        p2 = prev[SUBLANES - 2:SUBLANES - 1, :]
        g1 = pltpu.roll(gate, 1, 0)
        g2 = pltpu.roll(gate, 2, 0)
        g1 = jnp.concatenate([jnp.where(rid == 0, p1, g1[:SUBLANES]), g1[SUBLANES:]], axis=0)
        g2 = jnp.concatenate([jnp.where(rid == 0, p2, jnp.where(rid == 1, p1, g2[:SUBLANES])), g2[SUBLANES:]],
                             axis=0)
        conv = cb_ref[:, cs] + cw_ref[0:1, cs] * g2 + cw_ref[1:2, cs] * g1 + cw_ref[2:3, cs] * gate
        act_ref[:, cs] = (conv * _sigmoid(conv) * val).astype(BF16)
    h = x + _dot(act_ref[...], wdn_ref[...])
    if final_norm:
        h = _rms(h, fg_ref[...])
    out_ref[0] = h


def _ffn_layer(x, norm_g, w_up, conv_w, conv_b, w_down, final_g, final_norm):
    B, S, D = x.shape
    F = w_down.shape[0]
    T = min(FFN_TILE, S)
    assert S % T == 0 and F % FFN_FCHUNK == 0 and conv_w.shape[0] == CONV_WIDTH
    tile = pl.BlockSpec((1, T, D), lambda b, s: (b, s, 0))
    return pl.pallas_call(
        functools.partial(_ffn_kernel, final_norm=final_norm),
        grid=(B, S // T),
        in_specs=[tile, _resident((1, D)), _resident(w_up.shape), _resident(conv_w.shape),
                  _resident((1, F)), _resident(w_down.shape), _resident((1, D))],
        out_specs=tile,
        out_shape=jax.ShapeDtypeStruct(x.shape, F32),
        scratch_shapes=[pltpu.VMEM((SUBLANES, F), F32), pltpu.VMEM((T, F), BF16)],
        compiler_params=pltpu.CompilerParams(dimension_semantics=("arbitrary", "arbitrary"),
                                             vmem_limit_bytes=VMEM_LIMIT),
        name="conv_ffn_final" if final_norm else "conv_ffn",
    )(x, norm_g.reshape(1, D), w_up, conv_w, conv_b.reshape(1, F), w_down, final_g.reshape(1, D))


def _attn_kernel(sink_ref, h_ref, ag_ref, kg_ref, wq_ref, wkv_ref, wo_ref, out_ref,
                 kprev_ref, vprev_ref, bias_s, attn_s):
    T, D = attn_s.shape
    W = WINDOW
    HD = ATT_HEAD_DIM
    KVD = ATT_KV_HEADS * HD
    pairs = ATT_GROUP // 2
    s_idx = pl.program_id(1)

    @pl.when(s_idx == 0)
    def _():
        kprev_ref[...] = jnp.zeros_like(kprev_ref)
        vprev_ref[...] = jnp.zeros_like(vprev_ref)

    h = h_ref[0]
    hn = h * lax.rsqrt(jnp.mean(h * h, axis=-1, keepdims=True) + EPS)
    q = (_dot((hn * ag_ref[...]).astype(BF16), wq_ref[...]) * (HD ** -0.5)).astype(BF16)
    kv = _dot((hn * kg_ref[...]).astype(BF16), wkv_ref[...])
    k_all = jnp.concatenate([kprev_ref[...], kv[:, :KVD]], axis=0)
    v_all = jnp.concatenate([vprev_ref[...], kv[:, KVD:]], axis=0)
    kprev_ref[...] = kv[T - W:, :KVD]
    vprev_ref[...] = kv[T - W:, KVD:]

    dist = (lax.broadcasted_iota(jnp.int32, (W, 2 * W), 0) + W
            - lax.broadcasted_iota(jnp.int32, (W, 2 * W), 1))
    valid = (dist >= 0) & (dist < W)
    distf = dist.astype(F32)
    for hq in range(ATT_Q_HEADS):
        slope = 2.0 ** (-8.0 * (hq + 1) / ATT_Q_HEADS)
        bias_s[hq] = jnp.where(valid, -slope * distf, -jnp.inf)
    first_mask = jnp.where(
        (lax.broadcasted_iota(jnp.int32, (1, 2 * W), 1) < W) & (s_idx == 0), -jnp.inf, 0.0)

    lane = lax.broadcasted_iota(jnp.int32, (1, KVD), 1)
    low = lane < HD
    k_rot = pltpu.roll(k_all, HD, 1)
    v_rot = pltpu.roll(v_all, HD, 1)

    def padded(t_all, t_rot, kvh, parity):
        src = t_all if kvh == parity else t_rot
        return jnp.where(low if parity == 0 else jnp.logical_not(low), src, 0.0).astype(BF16)

    for kvh in range(ATT_KV_HEADS):
        for parity in range(2):
            k_pad = padded(k_all, k_rot, kvh, parity)
            v_pad = padded(v_all, v_rot, kvh, parity)
            heads = [kvh * ATT_GROUP + 2 * m + parity for m in range(pairs)]
            bias = jnp.concatenate([bias_s[hq] for hq in heads], axis=0)
            sink = jnp.concatenate([jnp.full((W, 1), sink_ref[hq], F32) for hq in heads], axis=0)
            for n in range(T // W):
                rows = slice(n * W, (n + 1) * W)
                band = slice(n * W, (n + 2) * W)
                qs = jnp.concatenate(
                    [q[rows, (kvh * pairs + m) * LANES:(kvh * pairs + m + 1) * LANES] for m in range(pairs)],
                    axis=0)
                sc = _dot_nt(qs, k_pad[band]) + bias
                if n == 0:
                    sc = sc + first_mask
                mx = jnp.maximum(jnp.max(sc, axis=-1, keepdims=True), sink)
                e = jnp.exp(sc - mx)
                den = jnp.sum(e, axis=-1, keepdims=True) + jnp.exp(sink - mx)
                pv = _dot(e.astype(BF16), v_pad[band]) / den
                for m in range(pairs):
                    cols = slice((kvh * pairs + m) * LANES, (kvh * pairs + m + 1) * LANES)
                    piece = pv[m * W:(m + 1) * W]
                    if parity == 0:
                        attn_s[rows, cols] = piece
                    else:
                        attn_s[rows, cols] += piece

    out_ref[0] = h + _dot(attn_s[...].astype(BF16), wo_ref[...])


def _attn_layer(h, attn_g, kv_g, w_q, w_kv, sinks, w_o):
    B, S, D = h.shape
    T = min(ATT_TILE, S)
    assert S % T == 0 and T % WINDOW == 0
    assert w_q.shape[1] == ATT_Q_HEADS * ATT_HEAD_DIM and w_kv.shape[1] == 2 * ATT_KV_HEADS * ATT_HEAD_DIM
    tile = pl.BlockSpec((1, T, D), lambda b, s: (b, s, 0))
    kvd = ATT_KV_HEADS * ATT_HEAD_DIM
    return pl.pallas_call(
        _attn_kernel,
        grid=(B, S // T),
        in_specs=[pl.BlockSpec(memory_space=pltpu.SMEM), tile, _resident((1, D)), _resident((1, D)),
                  _resident(w_q.shape), _resident(w_kv.shape), _resident(w_o.shape)],
        out_specs=tile,
        out_shape=jax.ShapeDtypeStruct(h.shape, F32),
        scratch_shapes=[pltpu.VMEM((WINDOW, kvd), F32), pltpu.VMEM((WINDOW, kvd), F32),
                        pltpu.VMEM((ATT_Q_HEADS, WINDOW, 2 * WINDOW), F32),
                        pltpu.VMEM((T, ATT_Q_HEADS * ATT_HEAD_DIM), F32)],
        compiler_params=pltpu.CompilerParams(dimension_semantics=("arbitrary", "arbitrary"),
                                             vmem_limit_bytes=VMEM_LIMIT),
        name="swa_layer",
    )(sinks, h, attn_g.reshape(1, D), kv_g.reshape(1, D), w_q, w_kv, w_o)


def kernel(x, hg_norm, hg_w_in, hg_lb_logits, hg_out_norm, hg_w_out, kv_norm, w_kv, attn_norm, attn_w_q,
           attn_sinks, attn_w_o, ffn_norm, ffn_w_up, ffn_conv_w, ffn_conv_b, ffn_w_down, final_norm):
    depth = ffn_norm.shape[0]
    n_a = hg_norm.shape[0]
    assert depth - n_a == 1
    bf = lambda w: w.astype(BF16)
    h = x
    for layer in range(depth):
        if layer < n_a:
            h = _hgrn_layer(h, hg_norm[layer], hg_lb_logits, bf(hg_w_in[layer]), hg_out_norm[layer],
                            bf(hg_w_out[layer]), layer)
        else:
            bi = layer - n_a
            h = _attn_layer(h, attn_norm[bi], kv_norm, bf(attn_w_q[bi]), bf(w_kv), attn_sinks[bi],
                            bf(attn_w_o[bi]))
        h = _ffn_layer(h, ffn_norm[layer], bf(ffn_w_up[layer]), ffn_conv_w[layer], ffn_conv_b[layer],
                       bf(ffn_w_down[layer]), final_norm, layer == depth - 1)
    return h
```

```python
import functools

import jax
import jax.numpy as jnp
from jax import lax
from jax.experimental import pallas as pl
from jax.experimental.pallas import tpu as pltpu

F32 = jnp.float32
BF16 = jnp.bfloat16
EPS = 1e-6

HG_HEADS = 8
HG_DK = 128
ATT_HEAD_DIM = 64
ATT_Q_HEADS = 16
ATT_KV_HEADS = 2
ATT_GROUP = ATT_Q_HEADS // ATT_KV_HEADS
WINDOW = 128
CONV_WIDTH = 3

HG_TILE = 512
HG_CHUNK = 128
FFN_TILE = 512
FFN_FCHUNK = 256
ATT_TILE = 512
SAFE_LOG_DECAY = -80.0
VMEM_LIMIT = 56 * 1024 * 1024
SUBLANES = 8
LANES = 128


def _dot(a, b):
    return jnp.dot(a, b, preferred_element_type=F32)


def _dot_nt(a, b):
    return lax.dot_general(a, b, (((1,), (1,)), ((), ())), preferred_element_type=F32)


def _rms(x, g):
    ms = jnp.mean(x * x, axis=-1, keepdims=True)
    return x * lax.rsqrt(ms + EPS) * g


def _sigmoid(x):
    return 1.0 / (1.0 + jnp.exp(-x))


def _resident(shape):
    nd = len(shape)
    return pl.BlockSpec(shape, lambda *_: (0,) * nd, pipeline_mode=pl.Buffered(1))


def _hgrn_kernel(x_ref, g_ref, lbl_ref, win_ref, onorm_ref, wout_ref, out_ref,
                 st_ref, q_s, k_s, v_s, cum_s, oi_s, o_s, *, layer):
    T, D = q_s.shape
    C = HG_CHUNK
    n_chunks = T // C
    s_idx = pl.program_id(1)

    @pl.when(s_idx == 0)
    def _():
        st_ref[...] = jnp.zeros_like(st_ref)

    x = x_ref[0]
    xn = _rms(x, g_ref[...]).astype(BF16)

    lg = lbl_ref[...]
    le = jnp.exp(lg - jnp.max(lg, axis=0, keepdims=True))
    lb = jnp.sum(le[0:layer + 1], axis=0, keepdims=True) / jnp.sum(le, axis=0, keepdims=True)

    pq = _dot(xn, win_ref[:, 0:D])
    q_s[...] = pq * _sigmoid(pq) * (HG_DK ** -0.5)
    pf = _dot(xn, win_ref[:, D:2 * D])
    forget = lb + (1.0 - lb) * _sigmoid(pf)
    k_s[...] = 1.0 - forget
    logf = jnp.log(forget)
    v_s[...] = _dot(xn, win_ref[:, 2 * D:3 * D])
    pg = _dot(xn, win_ref[:, 3 * D:4 * D])
    gate = pg * _sigmoid(pg)

    row = lax.broadcasted_iota(jnp.int32, (C, C), 0)
    col = lax.broadcasted_iota(jnp.int32, (C, C), 1)
    causal = row >= col
    tri = jnp.where(causal, 1.0, 0.0).astype(BF16)
    l_hi = logf.astype(BF16)
    l_lo = (logf - l_hi.astype(F32)).astype(BF16)
    for c in range(n_chunks):
        r = slice(c * C, (c + 1) * C)
        cum_s[r, :] = _dot(tri, l_hi[r]) + _dot(tri, l_lo[r])

    safe = jnp.min(cum_s[...]) > SAFE_LOG_DECAY

    def recurrence(fast):
        states = [st_ref[h] for h in range(HG_HEADS)]
        for c in range(n_chunks):
            r = slice(c * C, (c + 1) * C)
            for h in range(HG_HEADS):
                hs = slice(h * HG_DK, (h + 1) * HG_DK)
                cum = cum_s[r, hs]
                last = cum[C - 1:C, :]
                decay = jnp.exp(last)
                qt = (q_s[r, hs] * jnp.exp(cum)).astype(BF16)
                vt = v_s[r, hs].T.astype(BF16)
                st_b = states[h].astype(BF16)
                if fast:
                    kt = k_s[r, hs] * jnp.exp(-cum)
                    a = jnp.where(causal, _dot_nt(qt, kt.astype(BF16)), 0.0).astype(BF16)
                    o_s[r, hs] = _dot_nt(jnp.concatenate([a, qt], axis=1), jnp.concatenate([vt, st_b], axis=1))
                    kd = (kt * decay).astype(BF16)
                else:
                    o_s[r, hs] = oi_s[r, hs] + _dot_nt(qt, st_b)
                    kd = (k_s[r, hs] * jnp.exp(last - cum)).astype(BF16)
                states[h] = states[h] * decay + _dot(vt, kd)
        for h in range(HG_HEADS):
            st_ref[h] = states[h]

    @pl.when(safe)
    def _():
        recurrence(True)

    @pl.when(jnp.logical_not(safe))
    def _():
        tcol = lax.broadcasted_iota(jnp.int32, (C, 1), 0)
        for c in range(n_chunks):
            r = slice(c * C, (c + 1) * C)
            qc = q_s[r, :]
            cumc = cum_s[r, :]

            def body(j, acc):
                kj = k_s[pl.ds(c * C + j, 1), :]
                vj = v_s[pl.ds(c * C + j, 1), :]
                cj = cum_s[pl.ds(c * C + j, 1), :]
                w = qc * kj * jnp.exp(jnp.minimum(cumc - cj, 0.0))
                keep = tcol >= j
                parts = []
                for h in range(HG_HEADS):
                    hs = slice(h * HG_DK, (h + 1) * HG_DK)
                    sc = jnp.sum(w[:, hs], axis=-1, keepdims=True)
                    parts.append(jnp.where(keep, sc, 0.0) * vj[:, hs])
                return acc + jnp.concatenate(parts, axis=-1)

            oi_s[r, :] = lax.fori_loop(0, C, body, jnp.zeros((C, D), F32))
        recurrence(False)

    onorm = onorm_ref[...]
    parts = []
    for h in range(HG_HEADS):
        hs = slice(h * HG_DK, (h + 1) * HG_DK)
        parts.append(_rms(o_s[:, hs], onorm) * gate[:, hs])
    o = jnp.concatenate(parts, axis=-1).astype(BF16)
    out_ref[0] = x + _dot(o, wout_ref[...])


def _hgrn_layer(x, norm_g, lb_logits, w_in, out_norm, w_out, layer):
    B, S, D = x.shape
    T = min(HG_TILE, S)
    assert S % T == 0 and T % HG_CHUNK == 0 and D == HG_HEADS * HG_DK
    tile = pl.BlockSpec((1, T, D), lambda b, s: (b, s, 0))
    return pl.pallas_call(
        functools.partial(_hgrn_kernel, layer=layer),
        grid=(B, S // T),
        in_specs=[tile, _resident((1, D)), _resident(lb_logits.shape), _resident(w_in.shape),
                  _resident((1, HG_DK)), _resident(w_out.shape)],
        out_specs=tile,
        out_shape=jax.ShapeDtypeStruct(x.shape, F32),
        scratch_shapes=[pltpu.VMEM((HG_HEADS, HG_DK, HG_DK), F32)] + [pltpu.VMEM((T, D), F32)] * 6,
        compiler_params=pltpu.CompilerParams(dimension_semantics=("arbitrary", "arbitrary"),
                                             vmem_limit_bytes=VMEM_LIMIT),
        name="hgrn2_layer",
    )(x, norm_g.reshape(1, D), lb_logits, w_in, out_norm.reshape(1, HG_DK), w_out)


def _ffn_kernel(x_ref, g_ref, wup_ref, cw_ref, cb_ref, wdn_ref, fg_ref, out_ref, carry_ref, act_ref,
                *, final_norm):
    T, F = act_ref.shape
    FC = FFN_FCHUNK
    s_idx = pl.program_id(1)
    x = x_ref[0]
    xn = _rms(x, g_ref[...]).astype(BF16)
    has_prev = s_idx > 0
    rid = lax.broadcasted_iota(jnp.int32, (SUBLANES, 1), 0)
    for j in range(F // FC):
        cs = slice(j * FC, (j + 1) * FC)
        gate = _dot(xn, wup_ref[:, cs])
        val = _dot(xn, wup_ref[:, F + j * FC:F + (j + 1) * FC])
        prev = jnp.where(has_prev, carry_ref[:, cs], 0.0)
        carry_ref[:, cs] = gate[T - SUBLANES:, :]
        p1 = prev[SUBLANES - 1:SUBLANES, :]
        p2 = prev[SUBLANES - 2:SUBLANES - 1, :]
        g1 = pltpu.roll(gate, 1, 0)
        g2 = pltpu.roll(gate, 2, 0)
        g1 = jnp.concatenate([jnp.where(rid == 0, p1, g1[:SUBLANES]), g1[SUBLANES:]], axis=0)
        g2 = jnp.concatenate([jnp.where(rid == 0, p2, jnp.where(rid == 1, p1, g2[:SUBLANES])), g2[SUBLANES:]],
                             axis=0)
        conv = cb_ref[:, cs] + cw_ref[0:1, cs] * g2 + cw_ref[1:2, cs] * g1 + cw_ref[2:3, cs] * gate
        act_ref[:, cs] = (conv * _sigmoid(conv) * val).astype(BF16)
    h = x + _dot(act_ref[...], wdn_ref[...])
    if final_norm:
        h = _rms(h, fg_ref[...])
    out_ref[0] = h


def _ffn_layer(x, norm_g, w_up, conv_w, conv_b, w_down, final_g, final_norm):
    B, S, D = x.shape
    F = w_down.shape[0]
    T = min(FFN_TILE, S)
    assert S % T == 0 and F % FFN_FCHUNK == 0 and conv_w.shape[0] == CONV_WIDTH
    tile = pl.BlockSpec((1, T, D), lambda b, s: (b, s, 0))
    return pl.pallas_call(
        functools.partial(_ffn_kernel, final_norm=final_norm),
        grid=(B, S // T),
        in_specs=[tile, _resident((1, D)), _resident(w_up.shape), _resident(conv_w.shape),
                  _resident((1, F)), _resident(w_down.shape), _resident((1, D))],
        out_specs=tile,
        out_shape=jax.ShapeDtypeStruct(x.shape, F32),
        scratch_shapes=[pltpu.VMEM((SUBLANES, F), F32), pltpu.VMEM((T, F), BF16)],
        compiler_params=pltpu.CompilerParams(dimension_semantics=("arbitrary", "arbitrary"),
                                             vmem_limit_bytes=VMEM_LIMIT),
        name="conv_ffn_final" if final_norm else "conv_ffn",
    )(x, norm_g.reshape(1, D), w_up, conv_w, conv_b.reshape(1, F), w_down, final_g.reshape(1, D))


def _attn_kernel(sink_ref, h_ref, ag_ref, kg_ref, wq_ref, wkv_ref, wo_ref, out_ref,
                 kprev_ref, vprev_ref, bias_s, attn_s):
    T, D = attn_s.shape
    W = WINDOW
    HD = ATT_HEAD_DIM
    KVD = ATT_KV_HEADS * HD
    pairs = ATT_GROUP // 2
    s_idx = pl.program_id(1)

    @pl.when(s_idx == 0)
    def _():
        kprev_ref[...] = jnp.zeros_like(kprev_ref)
        vprev_ref[...] = jnp.zeros_like(vprev_ref)

    h = h_ref[0]
    hn = h * lax.rsqrt(jnp.mean(h * h, axis=-1, keepdims=True) + EPS)
    q = (_dot((hn * ag_ref[...]).astype(BF16), wq_ref[...]) * (HD ** -0.5)).astype(BF16)
    kv = _dot((hn * kg_ref[...]).astype(BF16), wkv_ref[...])
    k_all = jnp.concatenate([kprev_ref[...], kv[:, :KVD]], axis=0)
    v_all = jnp.concatenate([vprev_ref[...], kv[:, KVD:]], axis=0)
    kprev_ref[...] = kv[T - W:, :KVD]
    vprev_ref[...] = kv[T - W:, KVD:]

    dist = (lax.broadcasted_iota(jnp.int32, (W, 2 * W), 0) + W
            - lax.broadcasted_iota(jnp.int32, (W, 2 * W), 1))
    valid = (dist >= 0) & (dist < W)
    distf = dist.astype(F32)
    for hq in range(ATT_Q_HEADS):
        slope = 2.0 ** (-8.0 * (hq + 1) / ATT_Q_HEADS)
        bias_s[hq] = jnp.where(valid, -slope * distf, -jnp.inf)
    first_mask = jnp.where(
        (lax.broadcasted_iota(jnp.int32, (1, 2 * W), 1) < W) & (s_idx == 0), -jnp.inf, 0.0)

    lane = lax.broadcasted_iota(jnp.int32, (1, KVD), 1)
    low = lane < HD
    k_rot = pltpu.roll(k_all, HD, 1)
    v_rot = pltpu.roll(v_all, HD, 1)

    def padded(t_all, t_rot, kvh, parity):
        src = t_all if kvh == parity else t_rot
        return jnp.where(low if parity == 0 else jnp.logical_not(low), src, 0.0).astype(BF16)

    for kvh in range(ATT_KV_HEADS):
        for parity in range(2):
            k_pad = padded(k_all, k_rot, kvh, parity)
            v_pad = padded(v_all, v_rot, kvh, parity)
            heads = [kvh * ATT_GROUP + 2 * m + parity for m in range(pairs)]
            bias = jnp.concatenate([bias_s[hq] for hq in heads], axis=0)
            sink = jnp.concatenate([jnp.full((W, 1), sink_ref[hq], F32) for hq in heads], axis=0)
            for n in range(T // W):
                rows = slice(n * W, (n + 1) * W)
                band = slice(n * W, (n + 2) * W)
                qs = jnp.concatenate(
                    [q[rows, (kvh * pairs + m) * LANES:(kvh * pairs + m + 1) * LANES] for m in range(pairs)],
                    axis=0)
                sc = _dot_nt(qs, k_pad[band]) + bias
                if n == 0:
                    sc = sc + first_mask
                mx = jnp.maximum(jnp.max(sc, axis=-1, keepdims=True), sink)
                e = jnp.exp(sc - mx)
                den = jnp.sum(e, axis=-1, keepdims=True) + jnp.exp(sink - mx)
                pv = _dot(e.astype(BF16), v_pad[band]) / den
                for m in range(pairs):
                    cols = slice((kvh * pairs + m) * LANES, (kvh * pairs + m + 1) * LANES)
                    piece = pv[m * W:(m + 1) * W]
                    if parity == 0:
                        attn_s[rows, cols] = piece
                    else:
                        attn_s[rows, cols] += piece

    out_ref[0] = h + _dot(attn_s[...].astype(BF16), wo_ref[...])


def _attn_layer(h, attn_g, kv_g, w_q, w_kv, sinks, w_o):
    B, S, D = h.shape
    T = min(ATT_TILE, S)
    assert S % T == 0 and T % WINDOW == 0
    assert w_q.shape[1] == ATT_Q_HEADS * ATT_HEAD_DIM and w_kv.shape[1] == 2 * ATT_KV_HEADS * ATT_HEAD_DIM
    tile = pl.BlockSpec((1, T, D), lambda b, s: (b, s, 0))
    kvd = ATT_KV_HEADS * ATT_HEAD_DIM
    return pl.pallas_call(
        _attn_kernel,
        grid=(B, S // T),
        in_specs=[pl.BlockSpec(memory_space=pltpu.SMEM), tile, _resident((1, D)), _resident((1, D)),
                  _resident(w_q.shape), _resident(w_kv.shape), _resident(w_o.shape)],
        out_specs=tile,
        out_shape=jax.ShapeDtypeStruct(h.shape, F32),
        scratch_shapes=[pltpu.VMEM((WINDOW, kvd), F32), pltpu.VMEM((WINDOW, kvd), F32),
                        pltpu.VMEM((ATT_Q_HEADS, WINDOW, 2 * WINDOW), F32),
                        pltpu.VMEM((T, ATT_Q_HEADS * ATT_HEAD_DIM), F32)],
        compiler_params=pltpu.CompilerParams(dimension_semantics=("arbitrary", "arbitrary"),
                                             vmem_limit_bytes=VMEM_LIMIT),
        name="swa_layer",
    )(sinks, h, attn_g.reshape(1, D), kv_g.reshape(1, D), w_q, w_kv, w_o)


def kernel(x, hg_norm, hg_w_in, hg_lb_logits, hg_out_norm, hg_w_out, kv_norm, w_kv, attn_norm, attn_w_q,
           attn_sinks, attn_w_o, ffn_norm, ffn_w_up, ffn_conv_w, ffn_conv_b, ffn_w_down, final_norm):
    depth = ffn_norm.shape[0]
    n_a = hg_norm.shape[0]
    assert depth - n_a == 1
    bf = lambda w: w.astype(BF16)
    h = x
    for layer in range(depth):
        if layer < n_a:
            h = _hgrn_layer(h, hg_norm[layer], hg_lb_logits, bf(hg_w_in[layer]), hg_out_norm[layer],
                            bf(hg_w_out[layer]), layer)
        else:
            bi = layer - n_a
            h = _attn_layer(h, attn_norm[bi], kv_norm, bf(attn_w_q[bi]), bf(w_kv), attn_sinks[bi],
                            bf(attn_w_o[bi]))
        h = _ffn_layer(h, ffn_norm[layer], bf(ffn_w_up[layer]), ffn_conv_w[layer], ffn_conv_b[layer],
                       bf(ffn_w_down[layer]), final_norm, layer == depth - 1)
    return h
```

```python
import functools

import jax
import jax.numpy as jnp
from jax import lax
from jax.experimental import pallas as pl
from jax.experimental.pallas import tpu as pltpu

F32 = jnp.float32
BF16 = jnp.bfloat16
EPS = 1e-6

HG_HEADS = 8
HG_DK = 128
ATT_HEAD_DIM = 64
ATT_Q_HEADS = 16
ATT_KV_HEADS = 2
ATT_GROUP = ATT_Q_HEADS // ATT_KV_HEADS
WINDOW = 128
CONV_WIDTH = 3

HG_TILE = 512
HG_CHUNK = 128
FFN_TILE = 512
FFN_FCHUNK = 256
ATT_TILE = 512
SAFE_LOG_DECAY = -80.0
VMEM_LIMIT = 56 * 1024 * 1024
SUBLANES = 8
LANES = 128


def _dot(a, b):
    return jnp.dot(a, b, preferred_element_type=F32)


def _dot_nt(a, b):
    return lax.dot_general(a, b, (((1,), (1,)), ((), ())), preferred_element_type=F32)


def _rms(x, g):
    ms = jnp.mean(x * x, axis=-1, keepdims=True)
    return x * lax.rsqrt(ms + EPS) * g


def _sigmoid(x):
    return 1.0 / (1.0 + jnp.exp(-x))


def _resident(shape):
    nd = len(shape)
    return pl.BlockSpec(shape, lambda *_: (0,) * nd, pipeline_mode=pl.Buffered(1))


def _hgrn_kernel(x_ref, g_ref, lbl_ref, win_ref, onorm_ref, wout_ref, out_ref,
                 st_ref, q_s, k_s, v_s, cum_s, oi_s, o_s, *, layer):
    T, D = q_s.shape
    C = HG_CHUNK
    n_chunks = T // C
    s_idx = pl.program_id(1)

    @pl.when(s_idx == 0)
    def _():
        st_ref[...] = jnp.zeros_like(st_ref)

    x = x_ref[0]
    xn = _rms(x, g_ref[...]).astype(BF16)

    lg = lbl_ref[...]
    le = jnp.exp(lg - jnp.max(lg, axis=0, keepdims=True))
    lb = jnp.sum(le[0:layer + 1], axis=0, keepdims=True) / jnp.sum(le, axis=0, keepdims=True)

    pq = _dot(xn, win_ref[:, 0:D])
    q_s[...] = pq * _sigmoid(pq) * (HG_DK ** -0.5)
    pf = _dot(xn, win_ref[:, D:2 * D])
    forget = lb + (1.0 - lb) * _sigmoid(pf)
    k_s[...] = 1.0 - forget
    logf = jnp.log(forget)
    v_s[...] = _dot(xn, win_ref[:, 2 * D:3 * D])
    pg = _dot(xn, win_ref[:, 3 * D:4 * D])
    gate = pg * _sigmoid(pg)

    row = lax.broadcasted_iota(jnp.int32, (C, C), 0)
    col = lax.broadcasted_iota(jnp.int32, (C, C), 1)
    causal = row >= col
    tri = jnp.where(causal, 1.0, 0.0).astype(BF16)
    l_hi = logf.astype(BF16)
    l_lo = (logf - l_hi.astype(F32)).astype(BF16)
    for c in range(n_chunks):
        r = slice(c * C, (c + 1) * C)
        cum_s[r, :] = _dot(tri, l_hi[r]) + _dot(tri, l_lo[r])

    safe = jnp.min(cum_s[...]) > SAFE_LOG_DECAY

    def recurrence(fast):
        states = [st_ref[h] for h in range(HG_HEADS)]
        for c in range(n_chunks):
            r = slice(c * C, (c + 1) * C)
            for h in range(HG_HEADS):
                hs = slice(h * HG_DK, (h + 1) * HG_DK)
                cum = cum_s[r, hs]
                last = cum[C - 1:C, :]
                decay = jnp.exp(last)
                qt = (q_s[r, hs] * jnp.exp(cum)).astype(BF16)
                vt = v_s[r, hs].T.astype(BF16)
                st_b = states[h].astype(BF16)
                if fast:
                    kt = k_s[r, hs] * jnp.exp(-cum)
                    a = jnp.where(causal, _dot_nt(qt, kt.astype(BF16)), 0.0).astype(BF16)
                    o_s[r, hs] = _dot_nt(jnp.concatenate([a, qt], axis=1), jnp.concatenate([vt, st_b], axis=1))
                    kd = (kt * decay).astype(BF16)
                else:
                    o_s[r, hs] = oi_s[r, hs] + _dot_nt(qt, st_b)
                    kd = (k_s[r, hs] * jnp.exp(last - cum)).astype(BF16)
                states[h] = states[h] * decay + _dot(vt, kd)
        for h in range(HG_HEADS):
            st_ref[h] = states[h]

    @pl.when(safe)
    def _():
        recurrence(True)

    @pl.when(jnp.logical_not(safe))
    def _():
        tcol = lax.broadcasted_iota(jnp.int32, (C, 1), 0)
        for c in range(n_chunks):
            r = slice(c * C, (c + 1) * C)
            qc = q_s[r, :]
            cumc = cum_s[r, :]

            def body(j, acc):
                kj = k_s[pl.ds(c * C + j, 1), :]
                vj = v_s[pl.ds(c * C + j, 1), :]
                cj = cum_s[pl.ds(c * C + j, 1), :]
                w = qc * kj * jnp.exp(jnp.minimum(cumc - cj, 0.0))
                keep = tcol >= j
                parts = []
                for h in range(HG_HEADS):
                    hs = slice(h * HG_DK, (h + 1) * HG_DK)
                    sc = jnp.sum(w[:, hs], axis=-1, keepdims=True)
                    parts.append(jnp.where(keep, sc, 0.0) * vj[:, hs])
                return acc + jnp.concatenate(parts, axis=-1)

            oi_s[r, :] = lax.fori_loop(0, C, body, jnp.zeros((C, D), F32))
        recurrence(False)

    onorm = onorm_ref[...]
    parts = []
    for h in range(HG_HEADS):
        hs = slice(h * HG_DK, (h + 1) * HG_DK)
        parts.append(_rms(o_s[:, hs], onorm) * gate[:, hs])
    o = jnp.concatenate(parts, axis=-1).astype(BF16)
    out_ref[0] = x + _dot(o, wout_ref[...])


def _hgrn_layer(x, norm_g, lb_logits, w_in, out_norm, w_out, layer):
    B, S, D = x.shape
    T = min(HG_TILE, S)
    assert S % T == 0 and T % HG_CHUNK == 0 and D == HG_HEADS * HG_DK
    tile = pl.BlockSpec((1, T, D), lambda b, s: (b, s, 0))
    return pl.pallas_call(
        functools.partial(_hgrn_kernel, layer=layer),
        grid=(B, S // T),
        in_specs=[tile, _resident((1, D)), _resident(lb_logits.shape), _resident(w_in.shape),
                  _resident((1, HG_DK)), _resident(w_out.shape)],
        out_specs=tile,
        out_shape=jax.ShapeDtypeStruct(x.shape, F32),
        scratch_shapes=[pltpu.VMEM((HG_HEADS, HG_DK, HG_DK), F32)] + [pltpu.VMEM((T, D), F32)] * 6,
        compiler_params=pltpu.CompilerParams(dimension_semantics=("arbitrary", "arbitrary"),
                                             vmem_limit_bytes=VMEM_LIMIT),
        name="hgrn2_layer",
    )(x, norm_g.reshape(1, D), lb_logits, w_in, out_norm.reshape(1, HG_DK), w_out)


def _ffn_kernel(x_ref, g_ref, wup_ref, cw_ref, cb_ref, wdn_ref, fg_ref, out_ref, carry_ref, act_ref,
                *, final_norm):
    T, F = act_ref.shape
    FC = FFN_FCHUNK
    s_idx = pl.program_id(1)
    x = x_ref[0]
    xn = _rms(x, g_ref[...]).astype(BF16)
    has_prev = s_idx > 0
    rid = lax.broadcasted_iota(jnp.int32, (SUBLANES, 1), 0)
    for j in range(F // FC):
        cs = slice(j * FC, (j + 1) * FC)
        gate = _dot(xn, wup_ref[:, cs])
        val = _dot(xn, wup_ref[:, F + j * FC:F + (j + 1) * FC])
        prev = jnp.where(has_prev, carry_ref[:, cs], 0.0)
        carry_ref[:, cs] = gate[T - SUBLANES:, :]
        p1 = prev[SUBLANES - 1:SUBLANES, :]
        p2 = prev[SUBLANES - 2:SUBLANES - 1, :]
        g1 = pltpu.roll(gate, 1, 0)
        g2 = pltpu.roll(gate, 2, 0)
        g1 = jnp.concatenate([jnp.where(rid == 0, p1, g1[:SUBLANES]), g1[SUBLANES:]], axis=0)
        g2 = jnp.concatenate([jnp.where(rid == 0, p2, jnp.where(rid == 1, p1, g2[:SUBLANES])), g2[SUBLANES:]],
                             axis=0)
        conv = cb_ref[:, cs] + cw_ref[0:1, cs] * g2 + cw_ref[1:2, cs] * g1 + cw_ref[2:3, cs] * gate
        act_ref[:, cs] = (conv * _sigmoid(conv) * val).astype(BF16)
    h = x + _dot(act_ref[...], wdn_ref[...])
    if final_norm:
        h = _rms(h, fg_ref[...])
    out_ref[0] = h


def _ffn_layer(x, norm_g, w_up, conv_w, conv_b, w_down, final_g, final_norm):
    B, S, D = x.shape
    F = w_down.shape[0]
    T = min(FFN_TILE, S)
    assert S % T == 0 and F % FFN_FCHUNK == 0 and conv_w.shape[0] == CONV_WIDTH
    tile = pl.BlockSpec((1, T, D), lambda b, s: (b, s, 0))
    return pl.pallas_call(
        functools.partial(_ffn_kernel, final_norm=final_norm),
        grid=(B, S // T),
        in_specs=[tile, _resident((1, D)), _resident(w_up.shape), _resident(conv_w.shape),
                  _resident((1, F)), _resident(w_down.shape), _resident((1, D))],
        out_specs=tile,
        out_shape=jax.ShapeDtypeStruct(x.shape, F32),
        scratch_shapes=[pltpu.VMEM((SUBLANES, F), F32), pltpu.VMEM((T, F), BF16)],
        compiler_params=pltpu.CompilerParams(dimension_semantics=("arbitrary", "arbitrary"),
                                             vmem_limit_bytes=VMEM_LIMIT),
        name="conv_ffn_final" if final_norm else "conv_ffn",
    )(x, norm_g.reshape(1, D), w_up, conv_w, conv_b.reshape(1, F), w_down, final_g.reshape(1, D))


def _attn_kernel(sink_ref, h_ref, ag_ref, kg_ref, wq_ref, wkv_ref, wo_ref, out_ref,
                 kprev_ref, vprev_ref, bias_s, sinkb_s, attn_s):
    T, D = attn_s.shape
    W = WINDOW
    HD = ATT_HEAD_DIM
    KVD = ATT_KV_HEADS * HD
    pairs = ATT_GROUP // 2
    s_idx = pl.program_id(1)

    @pl.when(s_idx == 0)
    def _():
        kprev_ref[...] = jnp.zeros_like(kprev_ref)
        vprev_ref[...] = jnp.zeros_like(vprev_ref)

    h = h_ref[0]
    hn = h * lax.rsqrt(jnp.mean(h * h, axis=-1, keepdims=True) + EPS)
    q = (_dot((hn * ag_ref[...]).astype(BF16), wq_ref[...]) * (HD ** -0.5)).astype(BF16)
    kv = _dot((hn * kg_ref[...]).astype(BF16), wkv_ref[...])
    k_all = jnp.concatenate([kprev_ref[...], kv[:, :KVD]], axis=0)
    v_all = jnp.concatenate([vprev_ref[...], kv[:, KVD:]], axis=0)
    kprev_ref[...] = kv[T - W:, :KVD]
    vprev_ref[...] = kv[T - W:, KVD:]

    R = pairs * W
    ii = lax.broadcasted_iota(jnp.int32, (R, W), 0) & (W - 1)
    jj = lax.broadcasted_iota(jnp.int32, (R, W), 1)
    upper = jj > ii
    dist = jnp.where(upper, ii - jj + W, ii - jj).astype(F32)
    piece = lax.broadcasted_iota(jnp.int32, (R, W), 0) // W
    for kvh in range(ATT_KV_HEADS):
        for parity in range(2):
            slope = jnp.zeros((R, W), F32)
            sink = jnp.zeros((R, W), F32)
            for m in range(pairs):
                hq = kvh * ATT_GROUP + 2 * m + parity
                slope = jnp.where(piece == m, 2.0 ** (-8.0 * (hq + 1) / ATT_Q_HEADS), slope)
                sink = jnp.where(piece == m, sink_ref[hq], sink)
            bias_s[2 * kvh + parity] = -slope * dist
            sinkb_s[2 * kvh + parity] = sink
    first_mask = jnp.where(upper & (s_idx == 0), -jnp.inf, 0.0)

    lane = lax.broadcasted_iota(jnp.int32, (1, KVD), 1)
    low = lane < HD
    high = jnp.logical_not(low)
    k_rot = pltpu.roll(k_all, HD, 1)
    v_rot = pltpu.roll(v_all, HD, 1)

    def padded(t_all, t_rot, kvh, parity):
        src = t_all if kvh == parity else t_rot
        return jnp.where(low if parity == 0 else high, src, 0.0).astype(BF16)

    ones = [jnp.broadcast_to(jnp.where(sel, 1.0, 0.0), (2 * W, KVD)).astype(BF16) for sel in (low, high)]

    for kvh in range(ATT_KV_HEADS):
        k_pad = [padded(k_all, k_rot, kvh, parity) for parity in range(2)]
        v_pad = [padded(v_all, v_rot, kvh, parity) for parity in range(2)]
        for n in range(T // W):
            rows = slice(n * W, (n + 1) * W)
            band = slice(n * W, (n + 2) * W)
            qs = jnp.concatenate(
                [q[rows, (kvh * pairs + m) * LANES:(kvh * pairs + m + 1) * LANES] for m in range(pairs)],
                axis=0)
            probs, sink_terms = [], []
            for parity in range(2):
                sc = _dot_nt(qs, k_pad[parity][band])
                f = jnp.where(upper, sc[:, :W], sc[:, W:]) + bias_s[2 * kvh + parity]
                if n == 0:
                    f = f + first_mask
                mx = jnp.broadcast_to(jnp.max(f, axis=-1, keepdims=True), (R, W))
                e = jnp.exp(f - mx)
                sink_terms.append(jnp.exp(sinkb_s[2 * kvh + parity] - mx))
                probs.append(jnp.concatenate([jnp.where(upper, e, 0.0), jnp.where(upper, 0.0, e)],
                                             axis=1).astype(BF16))
            lhs = jnp.concatenate(probs, axis=1)
            rhs = jnp.concatenate(
                [jnp.concatenate([v_pad[parity][band], ones[parity]], axis=1) for parity in range(2)],
                axis=0)
            pv = _dot(lhs, rhs)
            out = pv[:, :KVD] / (pv[:, KVD:] + jnp.where(low, sink_terms[0], sink_terms[1]))
            for m in range(pairs):
                cols = slice((kvh * pairs + m) * LANES, (kvh * pairs + m + 1) * LANES)
                attn_s[rows, cols] = out[m * W:(m + 1) * W]

    out_ref[0] = h + _dot(attn_s[...].astype(BF16), wo_ref[...])


def _attn_layer(h, attn_g, kv_g, w_q, w_kv, sinks, w_o):
    B, S, D = h.shape
    T = min(ATT_TILE, S)
    assert S % T == 0 and T % WINDOW == 0
    assert w_q.shape[1] == ATT_Q_HEADS * ATT_HEAD_DIM and w_kv.shape[1] == 2 * ATT_KV_HEADS * ATT_HEAD_DIM
    tile = pl.BlockSpec((1, T, D), lambda b, s: (b, s, 0))
    kvd = ATT_KV_HEADS * ATT_HEAD_DIM
    return pl.pallas_call(
        _attn_kernel,
        grid=(B, S // T),
        in_specs=[pl.BlockSpec(memory_space=pltpu.SMEM), tile, _resident((1, D)), _resident((1, D)),
                  _resident(w_q.shape), _resident(w_kv.shape), _resident(w_o.shape)],
        out_specs=tile,
        out_shape=jax.ShapeDtypeStruct(h.shape, F32),
        scratch_shapes=[pltpu.VMEM((WINDOW, kvd), F32), pltpu.VMEM((WINDOW, kvd), F32),
                        pltpu.VMEM((2 * ATT_KV_HEADS, ATT_GROUP // 2 * WINDOW, WINDOW), F32),
                        pltpu.VMEM((2 * ATT_KV_HEADS, ATT_GROUP // 2 * WINDOW, WINDOW), F32),
                        pltpu.VMEM((T, ATT_Q_HEADS * ATT_HEAD_DIM), F32)],
        compiler_params=pltpu.CompilerParams(dimension_semantics=("arbitrary", "arbitrary"),
                                             vmem_limit_bytes=VMEM_LIMIT),
        name="swa_layer",
    )(sinks, h, attn_g.reshape(1, D), kv_g.reshape(1, D), w_q, w_kv, w_o)


def kernel(x, hg_norm, hg_w_in, hg_lb_logits, hg_out_norm, hg_w_out, kv_norm, w_kv, attn_norm, attn_w_q,
           attn_sinks, attn_w_o, ffn_norm, ffn_w_up, ffn_conv_w, ffn_conv_b, ffn_w_down, final_norm):
    depth = ffn_norm.shape[0]
    n_a = hg_norm.shape[0]
    assert depth - n_a == 1
    bf = lambda w: w.astype(BF16)
    h = x
    for layer in range(depth):
        if layer < n_a:
            h = _hgrn_layer(h, hg_norm[layer], hg_lb_logits, bf(hg_w_in[layer]), hg_out_norm[layer],
                            bf(hg_w_out[layer]), layer)
        else:
            bi = layer - n_a
            h = _attn_layer(h, attn_norm[bi], kv_norm, bf(attn_w_q[bi]), bf(w_kv), attn_sinks[bi],
                            bf(attn_w_o[bi]))
        h = _ffn_layer(h, ffn_norm[layer], bf(ffn_w_up[layer]), ffn_conv_w[layer], ffn_conv_b[layer],
                       bf(ffn_w_down[layer]), final_norm, layer == depth - 1)
    return h
```

```python
import functools

import jax
import jax.numpy as jnp
from jax import lax
from jax.experimental import pallas as pl
from jax.experimental.pallas import tpu as pltpu

F32 = jnp.float32
BF16 = jnp.bfloat16
EPS = 1e-6

HG_HEADS = 8
HG_DK = 128
ATT_HEAD_DIM = 64
ATT_Q_HEADS = 16
ATT_KV_HEADS = 2
ATT_GROUP = ATT_Q_HEADS // ATT_KV_HEADS
WINDOW = 128
CONV_WIDTH = 3

HG_TILE = 512
HG_CHUNK = 128
FFN_TILE = 512
FFN_FCHUNK = 256
ATT_TILE = 512
SAFE_LOG_DECAY = -80.0
VMEM_LIMIT = 56 * 1024 * 1024
SUBLANES = 8
LANES = 128


def _dot(a, b):
    return jnp.dot(a, b, preferred_element_type=F32)


def _dot_nt(a, b):
    return lax.dot_general(a, b, (((1,), (1,)), ((), ())), preferred_element_type=F32)


def _rms(x, g):
    ms = jnp.mean(x * x, axis=-1, keepdims=True)
    return x * lax.rsqrt(ms + EPS) * g


def _sigmoid(x):
    return 1.0 / (1.0 + jnp.exp(-x))


def _resident(shape):
    nd = len(shape)
    return pl.BlockSpec(shape, lambda *_: (0,) * nd, pipeline_mode=pl.Buffered(1))


def _hgrn_kernel(x_ref, g_ref, lbl_ref, win_ref, onorm_ref, wout_ref, out_ref,
                 st_ref, q_s, k_s, v_s, cum_s, oi_s, o_s, qt_s, a_s, vt_s, inc_s, *, layer):
    T, D = q_s.shape
    C = HG_CHUNK
    n_chunks = T // C
    s_idx = pl.program_id(1)

    @pl.when(s_idx == 0)
    def _():
        st_ref[...] = jnp.zeros_like(st_ref)

    x = x_ref[0]
    xn = _rms(x, g_ref[...]).astype(BF16)

    lg = lbl_ref[...]
    le = jnp.exp(lg - jnp.max(lg, axis=0, keepdims=True))
    lb = jnp.sum(le[0:layer + 1], axis=0, keepdims=True) / jnp.sum(le, axis=0, keepdims=True)

    pq = _dot(xn, win_ref[:, 0:D])
    q_s[...] = pq * _sigmoid(pq) * (HG_DK ** -0.5)
    pf = _dot(xn, win_ref[:, D:2 * D])
    forget = lb + (1.0 - lb) * _sigmoid(pf)
    k_s[...] = 1.0 - forget
    logf = jnp.log(forget)
    v_s[...] = _dot(xn, win_ref[:, 2 * D:3 * D])
    pg = _dot(xn, win_ref[:, 3 * D:4 * D])
    gate = pg * _sigmoid(pg)

    row = lax.broadcasted_iota(jnp.int32, (C, C), 0)
    col = lax.broadcasted_iota(jnp.int32, (C, C), 1)
    causal = row >= col
    tri = jnp.where(causal, 1.0, 0.0).astype(BF16)
    l_hi = logf.astype(BF16)
    l_lo = (logf - l_hi.astype(F32)).astype(BF16)
    for c in range(n_chunks):
        r = slice(c * C, (c + 1) * C)
        cum_s[r, :] = _dot(tri, l_hi[r]) + _dot(tri, l_lo[r])

    safe = jnp.min(cum_s[...]) > SAFE_LOG_DECAY

    def stage(fast):
        for c in range(n_chunks):
            r = slice(c * C, (c + 1) * C)
            for h in range(HG_HEADS):
                hs = slice(h * HG_DK, (h + 1) * HG_DK)
                cum = cum_s[r, hs]
                last = cum[C - 1:C, :]
                qt = (q_s[r, hs] * jnp.exp(cum)).astype(BF16)
                vt = v_s[r, hs].T.astype(BF16)
                qt_s[r, hs] = qt
                vt_s[c * HG_HEADS + h] = vt
                if fast:
                    kt = k_s[r, hs] * jnp.exp(-cum)
                    a_s[r, hs] = jnp.where(causal, _dot_nt(qt, kt.astype(BF16)), 0.0).astype(BF16)
                    kd = (kt * jnp.exp(last)).astype(BF16)
                else:
                    kd = (k_s[r, hs] * jnp.exp(last - cum)).astype(BF16)
                inc_s[c * HG_HEADS + h] = _dot(vt, kd)

    def scan(fast):
        states = [st_ref[h] for h in range(HG_HEADS)]
        for c in range(n_chunks):
            r = slice(c * C, (c + 1) * C)
            for h in range(HG_HEADS):
                hs = slice(h * HG_DK, (h + 1) * HG_DK)
                st_b = states[h].astype(BF16)
                if fast:
                    o_s[r, hs] = _dot_nt(jnp.concatenate([a_s[r, hs], qt_s[r, hs]], axis=1),
                                         jnp.concatenate([vt_s[c * HG_HEADS + h], st_b], axis=1))
                else:
                    o_s[r, hs] = oi_s[r, hs] + _dot_nt(qt_s[r, hs], st_b)
                decay = jnp.exp(cum_s[(c + 1) * C - 1:(c + 1) * C, hs])
                states[h] = states[h] * decay + inc_s[c * HG_HEADS + h]
        for h in range(HG_HEADS):
            st_ref[h] = states[h]

    @pl.when(safe)
    def _():
        stage(True)
        scan(True)

    @pl.when(jnp.logical_not(safe))
    def _():
        tcol = lax.broadcasted_iota(jnp.int32, (C, 1), 0)
        for c in range(n_chunks):
            r = slice(c * C, (c + 1) * C)
            qc = q_s[r, :]
            cumc = cum_s[r, :]

            def body(j, acc):
                kj = k_s[pl.ds(c * C + j, 1), :]
                vj = v_s[pl.ds(c * C + j, 1), :]
                cj = cum_s[pl.ds(c * C + j, 1), :]
                w = qc * kj * jnp.exp(jnp.minimum(cumc - cj, 0.0))
                keep = tcol >= j
                parts = []
                for h in range(HG_HEADS):
                    hs = slice(h * HG_DK, (h + 1) * HG_DK)
                    sc = jnp.sum(w[:, hs], axis=-1, keepdims=True)
                    parts.append(jnp.where(keep, sc, 0.0) * vj[:, hs])
                return acc + jnp.concatenate(parts, axis=-1)

            oi_s[r, :] = lax.fori_loop(0, C, body, jnp.zeros((C, D), F32))
        stage(False)
        scan(False)

    onorm = onorm_ref[...]
    parts = []
    for h in range(HG_HEADS):
        hs = slice(h * HG_DK, (h + 1) * HG_DK)
        parts.append(_rms(o_s[:, hs], onorm) * gate[:, hs])
    o = jnp.concatenate(parts, axis=-1).astype(BF16)
    out_ref[0] = x + _dot(o, wout_ref[...])


def _hgrn_layer(x, norm_g, lb_logits, w_in, out_norm, w_out, layer):
    B, S, D = x.shape
    T = min(HG_TILE, S)
    assert S % T == 0 and T % HG_CHUNK == 0 and D == HG_HEADS * HG_DK
    tile = pl.BlockSpec((1, T, D), lambda b, s: (b, s, 0))
    return pl.pallas_call(
        functools.partial(_hgrn_kernel, layer=layer),
        grid=(B, S // T),
        in_specs=[tile, _resident((1, D)), _resident(lb_logits.shape), _resident(w_in.shape),
                  _resident((1, HG_DK)), _resident(w_out.shape)],
        out_specs=tile,
        out_shape=jax.ShapeDtypeStruct(x.shape, F32),
        scratch_shapes=([pltpu.VMEM((HG_HEADS, HG_DK, HG_DK), F32)] + [pltpu.VMEM((T, D), F32)] * 6
                        + [pltpu.VMEM((T, D), BF16)] * 2
                        + [pltpu.VMEM((T // HG_CHUNK * HG_HEADS, HG_DK, HG_CHUNK), BF16),
                           pltpu.VMEM((T // HG_CHUNK * HG_HEADS, HG_DK, HG_DK), F32)]),
        compiler_params=pltpu.CompilerParams(dimension_semantics=("arbitrary", "arbitrary"),
                                             vmem_limit_bytes=VMEM_LIMIT),
        name="hgrn2_layer",
    )(x, norm_g.reshape(1, D), lb_logits, w_in, out_norm.reshape(1, HG_DK), w_out)


def _ffn_kernel(x_ref, g_ref, wup_ref, cw_ref, cb_ref, wdn_ref, fg_ref, out_ref, carry_ref, act_ref,
                *, final_norm):
    T, F = act_ref.shape
    FC = FFN_FCHUNK
    s_idx = pl.program_id(1)
    x = x_ref[0]
    xn = _rms(x, g_ref[...]).astype(BF16)
    has_prev = s_idx > 0
    rid = lax.broadcasted_iota(jnp.int32, (SUBLANES, 1), 0)
    for j in range(F // FC):
        cs = slice(j * FC, (j + 1) * FC)
        gate = _dot(xn, wup_ref[:, cs])
        val = _dot(xn, wup_ref[:, F + j * FC:F + (j + 1) * FC])
        prev = jnp.where(has_prev, carry_ref[:, cs], 0.0)
        carry_ref[:, cs] = gate[T - SUBLANES:, :]
        p1 = prev[SUBLANES - 1:SUBLANES, :]
        p2 = prev[SUBLANES - 2:SUBLANES - 1, :]
        g1 = pltpu.roll(gate, 1, 0)
        g2 = pltpu.roll(gate, 2, 0)
        g1 = jnp.concatenate([jnp.where(rid == 0, p1, g1[:SUBLANES]), g1[SUBLANES:]], axis=0)
        g2 = jnp.concatenate([jnp.where(rid == 0, p2, jnp.where(rid == 1, p1, g2[:SUBLANES])), g2[SUBLANES:]],
                             axis=0)
        conv = cb_ref[:, cs] + cw_ref[0:1, cs] * g2 + cw_ref[1:2, cs] * g1 + cw_ref[2:3, cs] * gate
        act_ref[:, cs] = (conv * _sigmoid(conv) * val).astype(BF16)
    h = x + _dot(act_ref[...], wdn_ref[...])
    if final_norm:
        h = _rms(h, fg_ref[...])
    out_ref[0] = h


def _ffn_layer(x, norm_g, w_up, conv_w, conv_b, w_down, final_g, final_norm):
    B, S, D = x.shape
    F = w_down.shape[0]
    T = min(FFN_TILE, S)
    assert S % T == 0 and F % FFN_FCHUNK == 0 and conv_w.shape[0] == CONV_WIDTH
    tile = pl.BlockSpec((1, T, D), lambda b, s: (b, s, 0))
    return pl.pallas_call(
        functools.partial(_ffn_kernel, final_norm=final_norm),
        grid=(B, S // T),
        in_specs=[tile, _resident((1, D)), _resident(w_up.shape), _resident(conv_w.shape),
                  _resident((1, F)), _resident(w_down.shape), _resident((1, D))],
        out_specs=tile,
        out_shape=jax.ShapeDtypeStruct(x.shape, F32),
        scratch_shapes=[pltpu.VMEM((SUBLANES, F), F32), pltpu.VMEM((T, F), BF16)],
        compiler_params=pltpu.CompilerParams(dimension_semantics=("arbitrary", "arbitrary"),
                                             vmem_limit_bytes=VMEM_LIMIT),
        name="conv_ffn_final" if final_norm else "conv_ffn",
    )(x, norm_g.reshape(1, D), w_up, conv_w, conv_b.reshape(1, F), w_down, final_g.reshape(1, D))


def _attn_kernel(sink_ref, h_ref, ag_ref, kg_ref, wq_ref, wkv_ref, wo_ref, out_ref,
                 kprev_ref, vprev_ref, bias_s, sinkb_s, attn_s):
    T, D = attn_s.shape
    W = WINDOW
    HD = ATT_HEAD_DIM
    KVD = ATT_KV_HEADS * HD
    pairs = ATT_GROUP // 2
    s_idx = pl.program_id(1)

    @pl.when(s_idx == 0)
    def _():
        kprev_ref[...] = jnp.zeros_like(kprev_ref)
        vprev_ref[...] = jnp.zeros_like(vprev_ref)

    h = h_ref[0]
    hn = h * lax.rsqrt(jnp.mean(h * h, axis=-1, keepdims=True) + EPS)
    q = (_dot((hn * ag_ref[...]).astype(BF16), wq_ref[...]) * (HD ** -0.5)).astype(BF16)
    kv = _dot((hn * kg_ref[...]).astype(BF16), wkv_ref[...])
    k_all = jnp.concatenate([kprev_ref[...], kv[:, :KVD]], axis=0)
    v_all = jnp.concatenate([vprev_ref[...], kv[:, KVD:]], axis=0)
    kprev_ref[...] = kv[T - W:, :KVD]
    vprev_ref[...] = kv[T - W:, KVD:]

    R = pairs * W
    ii = lax.broadcasted_iota(jnp.int32, (R, W), 0) & (W - 1)
    jj = lax.broadcasted_iota(jnp.int32, (R, W), 1)
    upper = jj > ii
    dist = jnp.where(upper, ii - jj + W, ii - jj).astype(F32)
    piece = lax.broadcasted_iota(jnp.int32, (R, W), 0) // W
    for kvh in range(ATT_KV_HEADS):
        for parity in range(2):
            slope = jnp.zeros((R, W), F32)
            sink = jnp.zeros((R, W), F32)
            for m in range(pairs):
                hq = kvh * ATT_GROUP + 2 * m + parity
                slope = jnp.where(piece == m, 2.0 ** (-8.0 * (hq + 1) / ATT_Q_HEADS), slope)
                sink = jnp.where(piece == m, sink_ref[hq], sink)
            bias_s[2 * kvh + parity] = -slope * dist
            sinkb_s[2 * kvh + parity] = sink
    first_mask = jnp.where(upper & (s_idx == 0), -jnp.inf, 0.0)

    lane = lax.broadcasted_iota(jnp.int32, (1, KVD), 1)
    low = lane < HD
    high = jnp.logical_not(low)
    k_rot = pltpu.roll(k_all, HD, 1)
    v_rot = pltpu.roll(v_all, HD, 1)

    def padded(t_all, t_rot, kvh, parity):
        src = t_all if kvh == parity else t_rot
        return jnp.where(low if parity == 0 else high, src, 0.0).astype(BF16)

    ones = [jnp.broadcast_to(jnp.where(sel, 1.0, 0.0), (2 * W, KVD)).astype(BF16) for sel in (low, high)]

    for kvh in range(ATT_KV_HEADS):
        k_pad = [padded(k_all, k_rot, kvh, parity) for parity in range(2)]
        v_pad = [padded(v_all, v_rot, kvh, parity) for parity in range(2)]
        for n in range(T // W):
            rows = slice(n * W, (n + 1) * W)
            band = slice(n * W, (n + 2) * W)
            qs = jnp.concatenate(
                [q[rows, (kvh * pairs + m) * LANES:(kvh * pairs + m + 1) * LANES] for m in range(pairs)],
                axis=0)
            probs, sink_terms = [], []
            for parity in range(2):
                sc = _dot_nt(qs, k_pad[parity][band])
                f = jnp.where(upper, sc[:, :W], sc[:, W:]) + bias_s[2 * kvh + parity]
                if n == 0:
                    f = f + first_mask
                mx = jnp.broadcast_to(jnp.max(f, axis=-1, keepdims=True), (R, W))
                e = jnp.exp(f - mx)
                sink_terms.append(jnp.exp(sinkb_s[2 * kvh + parity] - mx))
                probs.append(jnp.concatenate([jnp.where(upper, e, 0.0), jnp.where(upper, 0.0, e)],
                                             axis=1).astype(BF16))
            lhs = jnp.concatenate(probs, axis=1)
            rhs = jnp.concatenate(
                [jnp.concatenate([v_pad[parity][band], ones[parity]], axis=1) for parity in range(2)],
                axis=0)
            pv = _dot(lhs, rhs)
            out = pv[:, :KVD] / (pv[:, KVD:] + jnp.where(low, sink_terms[0], sink_terms[1]))
            for m in range(pairs):
                cols = slice((kvh * pairs + m) * LANES, (kvh * pairs + m + 1) * LANES)
                attn_s[rows, cols] = out[m * W:(m + 1) * W]

    out_ref[0] = h + _dot(attn_s[...].astype(BF16), wo_ref[...])


def _attn_layer(h, attn_g, kv_g, w_q, w_kv, sinks, w_o):
    B, S, D = h.shape
    T = min(ATT_TILE, S)
    assert S % T == 0 and T % WINDOW == 0
    assert w_q.shape[1] == ATT_Q_HEADS * ATT_HEAD_DIM and w_kv.shape[1] == 2 * ATT_KV_HEADS * ATT_HEAD_DIM
    tile = pl.BlockSpec((1, T, D), lambda b, s: (b, s, 0))
    kvd = ATT_KV_HEADS * ATT_HEAD_DIM
    return pl.pallas_call(
        _attn_kernel,
        grid=(B, S // T),
        in_specs=[pl.BlockSpec(memory_space=pltpu.SMEM), tile, _resident((1, D)), _resident((1, D)),
                  _resident(w_q.shape), _resident(w_kv.shape), _resident(w_o.shape)],
        out_specs=tile,
        out_shape=jax.ShapeDtypeStruct(h.shape, F32),
        scratch_shapes=[pltpu.VMEM((WINDOW, kvd), F32), pltpu.VMEM((WINDOW, kvd), F32),
                        pltpu.VMEM((2 * ATT_KV_HEADS, ATT_GROUP // 2 * WINDOW, WINDOW), F32),
                        pltpu.VMEM((2 * ATT_KV_HEADS, ATT_GROUP // 2 * WINDOW, WINDOW), F32),
                        pltpu.VMEM((T, ATT_Q_HEADS * ATT_HEAD_DIM), F32)],
        compiler_params=pltpu.CompilerParams(dimension_semantics=("arbitrary", "arbitrary"),
                                             vmem_limit_bytes=VMEM_LIMIT),
        name="swa_layer",
    )(sinks, h, attn_g.reshape(1, D), kv_g.reshape(1, D), w_q, w_kv, w_o)


def kernel(x, hg_norm, hg_w_in, hg_lb_logits, hg_out_norm, hg_w_out, kv_norm, w_kv, attn_norm, attn_w_q,
           attn_sinks, attn_w_o, ffn_norm, ffn_w_up, ffn_conv_w, ffn_conv_b, ffn_w_down, final_norm):
    depth = ffn_norm.shape[0]
    n_a = hg_norm.shape[0]
    assert depth - n_a == 1
    bf = lambda w: w.astype(BF16)
    h = x
    for layer in range(depth):
        if layer < n_a:
            h = _hgrn_layer(h, hg_norm[layer], hg_lb_logits, bf(hg_w_in[layer]), hg_out_norm[layer],
                            bf(hg_w_out[layer]), layer)
        else:
            bi = layer - n_a
            h = _attn_layer(h, attn_norm[bi], kv_norm, bf(attn_w_q[bi]), bf(w_kv), attn_sinks[bi],
                            bf(attn_w_o[bi]))
        h = _ffn_layer(h, ffn_norm[layer], bf(ffn_w_up[layer]), ffn_conv_w[layer], ffn_conv_b[layer],
                       bf(ffn_w_down[layer]), final_norm, layer == depth - 1)
    return h
```

```python
import functools

import jax
import jax.numpy as jnp
from jax import lax
from jax.experimental import pallas as pl
from jax.experimental.pallas import tpu as pltpu

F32 = jnp.float32
BF16 = jnp.bfloat16
EPS = 1e-6

HG_HEADS = 8
HG_DK = 128
ATT_HEAD_DIM = 64
ATT_Q_HEADS = 16
ATT_KV_HEADS = 2
ATT_GROUP = ATT_Q_HEADS // ATT_KV_HEADS
WINDOW = 128
CONV_WIDTH = 3

HG_TILE = 512
HG_CHUNK = 128
FFN_TILE = 512
FFN_FCHUNK = 256
ATT_TILE = 512
SAFE_LOG_DECAY = -80.0
VMEM_LIMIT = 56 * 1024 * 1024
SUBLANES = 8
LANES = 128


def _dot(a, b):
    return lax.dot_general(a, b, (((1,), (0,)), ((), ())), preferred_element_type=F32)


def _dot_nt(a, b):
    return lax.dot_general(a, b, (((1,), (1,)), ((), ())), preferred_element_type=F32)


def _rms(x, g):
    ms = jnp.mean(x * x, axis=-1, keepdims=True)
    return x * lax.rsqrt(ms + EPS) * g


def _sigmoid(x):
    return 1.0 / (1.0 + jnp.exp(-x))


def _resident(shape):
    nd = len(shape)
    return pl.BlockSpec(shape, lambda *_: (0,) * nd, pipeline_mode=pl.Buffered(1))


def _resident_layer(shape, layer):
    nd = len(shape)
    return pl.BlockSpec((None,) + tuple(shape[1:]), lambda *_: (layer,) + (0,) * (nd - 1),
                        pipeline_mode=pl.Buffered(1))


def _hgrn_kernel(x_ref, g_ref, lbl_ref, win_ref, onorm_ref, wout_ref, out_ref,
                 st_ref, stn_ref, q_s, k_s, v_s, cum_s, oi_s, o_s, qt_s, a_s, vt_s, inc_s, *, layer):
    T, D = q_s.shape
    C = HG_CHUNK
    n_chunks = T // C
    s_idx = pl.program_id(1)

    @pl.when(s_idx == 0)
    def _():
        st_ref[...] = jnp.zeros_like(st_ref)

    x = x_ref[0]
    xn = _rms(x, g_ref[layer:layer + 1, :]).astype(BF16)

    lg = lbl_ref[...]
    le = jnp.exp(lg - jnp.max(lg, axis=0, keepdims=True))
    lb = jnp.sum(le[0:layer + 1], axis=0, keepdims=True) / jnp.sum(le, axis=0, keepdims=True)

    pq = _dot(xn, win_ref[:, 0:D])
    q_s[...] = pq * _sigmoid(pq) * (HG_DK ** -0.5)
    pf = _dot(xn, win_ref[:, D:2 * D])
    forget = lb + (1.0 - lb) * _sigmoid(pf)
    k_s[...] = 1.0 - forget
    logf = jnp.log(forget)
    v_s[...] = _dot(xn, win_ref[:, 2 * D:3 * D])
    pg = _dot(xn, win_ref[:, 3 * D:4 * D])
    gate = pg * _sigmoid(pg)

    row = lax.broadcasted_iota(jnp.int32, (C, C), 0)
    col = lax.broadcasted_iota(jnp.int32, (C, C), 1)
    causal = row >= col
    tri = jnp.where(causal, 1.0, 0.0).astype(BF16)
    l_hi = logf.astype(BF16)
    l_lo = (logf - l_hi.astype(F32)).astype(BF16)
    tri2 = jnp.concatenate([tri, tri], axis=1)
    for c in range(n_chunks):
        r = slice(c * C, (c + 1) * C)
        cum_s[r, :] = _dot(tri2, jnp.concatenate([l_hi[r], l_lo[r]], axis=0))

    lasts = jnp.concatenate([cum_s[(c + 1) * C - 1:(c + 1) * C, :] for c in range(n_chunks)], axis=0)
    safe = jnp.min(lasts) > SAFE_LOG_DECAY

    def stage(fast):
        for c in range(n_chunks):
            r = slice(c * C, (c + 1) * C)
            for h in range(HG_HEADS):
                hs = slice(h * HG_DK, (h + 1) * HG_DK)
                cum = cum_s[r, hs]
                last = cum[C - 1:C, :]
                qt = (q_s[r, hs] * jnp.exp(cum)).astype(BF16)
                vt = v_s[r, hs].T.astype(BF16)
                qt_s[r, hs] = qt
                vt_s[c * HG_HEADS + h] = vt
                if fast:
                    kt = k_s[r, hs] * jnp.exp(-cum)
                    a_s[r, hs] = jnp.where(causal, _dot_nt(qt, kt.astype(BF16)), 0.0).astype(BF16)
                    kd = (kt * jnp.exp(last)).astype(BF16)
                else:
                    kd = (k_s[r, hs] * jnp.exp(last - cum)).astype(BF16)
                inc_s[c * HG_HEADS + h] = _dot(vt, kd)

    def scan(fast):
        states = [st_ref[h] for h in range(HG_HEADS)]
        for c in range(n_chunks):
            r = slice(c * C, (c + 1) * C)
            for h in range(HG_HEADS):
                hs = slice(h * HG_DK, (h + 1) * HG_DK)
                st_b = states[h].astype(BF16)
                if fast:
                    o_s[r, hs] = _dot_nt(jnp.concatenate([a_s[r, hs], qt_s[r, hs]], axis=1),
                                         jnp.concatenate([vt_s[c * HG_HEADS + h], st_b], axis=1))
                else:
                    o_s[r, hs] = oi_s[r, hs] + _dot_nt(qt_s[r, hs], st_b)
                decay = jnp.exp(cum_s[(c + 1) * C - 1:(c + 1) * C, hs])
                states[h] = states[h] * decay + inc_s[c * HG_HEADS + h]
        for h in range(HG_HEADS):
            stn_ref[h] = states[h]

    def finish():
        onorm = onorm_ref[layer:layer + 1, :]
        parts = []
        for h in range(HG_HEADS):
            hs = slice(h * HG_DK, (h + 1) * HG_DK)
            parts.append(_rms(o_s[:, hs], onorm) * gate[:, hs])
        o = jnp.concatenate(parts, axis=-1).astype(BF16)
        out_ref[0] = x + _dot(o, wout_ref[...])

    stage(True)
    scan(True)
    finish()

    @pl.when(jnp.logical_not(safe))
    def _():
        tcol = lax.broadcasted_iota(jnp.int32, (C, 1), 0)
        for c in range(n_chunks):
            r = slice(c * C, (c + 1) * C)
            qc = q_s[r, :]
            cumc = cum_s[r, :]

            def body(j, acc):
                kj = k_s[pl.ds(c * C + j, 1), :]
                vj = v_s[pl.ds(c * C + j, 1), :]
                cj = cum_s[pl.ds(c * C + j, 1), :]
                w = qc * kj * jnp.exp(jnp.minimum(cumc - cj, 0.0))
                keep = tcol >= j
                parts = []
                for h in range(HG_HEADS):
                    hs = slice(h * HG_DK, (h + 1) * HG_DK)
                    sc = jnp.sum(w[:, hs], axis=-1, keepdims=True)
                    parts.append(jnp.where(keep, sc, 0.0) * vj[:, hs])
                return acc + jnp.concatenate(parts, axis=-1)

            oi_s[r, :] = lax.fori_loop(0, C, body, jnp.zeros((C, D), F32))
        stage(False)
        scan(False)
        finish()

    st_ref[...] = stn_ref[...]


def _hgrn_layer(x, norm_g, lb_logits, w_in, out_norm, w_out, layer):
    B, S, D = x.shape
    T = min(HG_TILE, S)
    assert S % T == 0 and T % HG_CHUNK == 0 and D == HG_HEADS * HG_DK
    tile = pl.BlockSpec((1, T, D), lambda b, s: (b, s, 0))
    return pl.pallas_call(
        functools.partial(_hgrn_kernel, layer=layer),
        grid=(B, S // T),
        in_specs=[tile, _resident(norm_g.shape), _resident(lb_logits.shape), _resident_layer(w_in.shape, layer),
                  _resident(out_norm.shape), _resident_layer(w_out.shape, layer)],
        out_specs=tile,
        out_shape=jax.ShapeDtypeStruct(x.shape, F32),
        scratch_shapes=([pltpu.VMEM((HG_HEADS, HG_DK, HG_DK), F32)] * 2 + [pltpu.VMEM((T, D), F32)] * 6
                        + [pltpu.VMEM((T, D), BF16)] * 2
                        + [pltpu.VMEM((T // HG_CHUNK * HG_HEADS, HG_DK, HG_CHUNK), BF16),
                           pltpu.VMEM((T // HG_CHUNK * HG_HEADS, HG_DK, HG_DK), F32)]),
        compiler_params=pltpu.CompilerParams(dimension_semantics=("arbitrary", "arbitrary"),
                                             vmem_limit_bytes=VMEM_LIMIT),
        name="hgrn2_layer",
    )(x, norm_g, lb_logits, w_in, out_norm, w_out)


def _ffn_kernel(x_ref, g_ref, wup_ref, cw_ref, cb_ref, wdn_ref, fg_ref, out_ref, carry_ref, act_ref,
                *, layer, final_norm):
    T, F = act_ref.shape
    FC = FFN_FCHUNK
    s_idx = pl.program_id(1)
    x = x_ref[0]
    xn = _rms(x, g_ref[layer:layer + 1, :]).astype(BF16)
    has_prev = s_idx > 0
    rid = lax.broadcasted_iota(jnp.int32, (SUBLANES, 1), 0)
    for j in range(F // FC):
        cs = slice(j * FC, (j + 1) * FC)
        gate = _dot(xn, wup_ref[:, cs])
        val = _dot(xn, wup_ref[:, F + j * FC:F + (j + 1) * FC])
        prev = jnp.where(has_prev, carry_ref[:, cs], 0.0)
        carry_ref[:, cs] = gate[T - SUBLANES:, :]
        p1 = prev[SUBLANES - 1:SUBLANES, :]
        p2 = prev[SUBLANES - 2:SUBLANES - 1, :]
        g1 = pltpu.roll(gate, 1, 0)
        g2 = pltpu.roll(gate, 2, 0)
        g1 = jnp.concatenate([jnp.where(rid == 0, p1, g1[:SUBLANES]), g1[SUBLANES:]], axis=0)
        g2 = jnp.concatenate([jnp.where(rid == 0, p2, jnp.where(rid == 1, p1, g2[:SUBLANES])), g2[SUBLANES:]],
                             axis=0)
        conv = cb_ref[layer:layer + 1, cs] + cw_ref[0:1, cs] * g2 + cw_ref[1:2, cs] * g1 + cw_ref[2:3, cs] * gate
        act_ref[:, cs] = (conv * _sigmoid(conv) * val).astype(BF16)
    h = x + _dot(act_ref[...], wdn_ref[...])
    if final_norm:
        h = _rms(h, fg_ref[...])
    out_ref[0] = h


def _ffn_layer(x, norm_g, w_up, conv_w, conv_b, w_down, final_g, layer, final_norm):
    B, S, D = x.shape
    F = w_down.shape[1]
    T = min(FFN_TILE, S)
    assert S % T == 0 and F % FFN_FCHUNK == 0 and conv_w.shape[1] == CONV_WIDTH
    tile = pl.BlockSpec((1, T, D), lambda b, s: (b, s, 0))
    return pl.pallas_call(
        functools.partial(_ffn_kernel, layer=layer, final_norm=final_norm),
        grid=(B, S // T),
        in_specs=[tile, _resident(norm_g.shape), _resident_layer(w_up.shape, layer),
                  _resident_layer(conv_w.shape, layer), _resident(conv_b.shape),
                  _resident_layer(w_down.shape, layer), _resident((1, D))],
        out_specs=tile,
        out_shape=jax.ShapeDtypeStruct(x.shape, F32),
        scratch_shapes=[pltpu.VMEM((SUBLANES, F), F32), pltpu.VMEM((T, F), BF16)],
        compiler_params=pltpu.CompilerParams(dimension_semantics=("arbitrary", "arbitrary"),
                                             vmem_limit_bytes=VMEM_LIMIT),
        name="conv_ffn_final" if final_norm else "conv_ffn",
    )(x, norm_g, w_up, conv_w, conv_b, w_down, final_g.reshape(1, D))


def _attn_kernel(sink_ref, h_ref, ag_ref, kg_ref, wq_ref, wkv_ref, wo_ref, out_ref,
                 kprev_ref, vprev_ref, bias_s, sinkb_s, attn_s, *, layer):
    T, D = attn_s.shape
    W = WINDOW
    HD = ATT_HEAD_DIM
    KVD = ATT_KV_HEADS * HD
    pairs = ATT_GROUP // 2
    s_idx = pl.program_id(1)

    @pl.when(s_idx == 0)
    def _():
        kprev_ref[...] = jnp.zeros_like(kprev_ref)
        vprev_ref[...] = jnp.zeros_like(vprev_ref)

    h = h_ref[0]
    hn = h * lax.rsqrt(jnp.mean(h * h, axis=-1, keepdims=True) + EPS)
    q = (_dot((hn * ag_ref[layer:layer + 1, :]).astype(BF16), wq_ref[...]) * (HD ** -0.5)).astype(BF16)
    kv = _dot((hn * kg_ref[...]).astype(BF16), wkv_ref[...])
    k_all = jnp.concatenate([kprev_ref[...], kv[:, :KVD]], axis=0)
    v_all = jnp.concatenate([vprev_ref[...], kv[:, KVD:]], axis=0)
    kprev_ref[...] = kv[T - W:, :KVD]
    vprev_ref[...] = kv[T - W:, KVD:]

    R = pairs * W
    ii = lax.broadcasted_iota(jnp.int32, (R, W), 0) & (W - 1)
    jj = lax.broadcasted_iota(jnp.int32, (R, W), 1)
    upper = jj > ii
    dist = jnp.where(upper, ii - jj + W, ii - jj).astype(F32)
    piece = lax.broadcasted_iota(jnp.int32, (R, W), 0) // W
    for kvh in range(ATT_KV_HEADS):
        for parity in range(2):
            slope = jnp.zeros((R, W), F32)
            sink = jnp.zeros((R, W), F32)
            for m in range(pairs):
                hq = kvh * ATT_GROUP + 2 * m + parity
                slope = jnp.where(piece == m, 2.0 ** (-8.0 * (hq + 1) / ATT_Q_HEADS), slope)
                sink = jnp.where(piece == m, sink_ref[layer, hq], sink)
            bias_s[2 * kvh + parity] = -slope * dist
            sinkb_s[2 * kvh + parity] = sink
    first_mask = jnp.where(upper & (s_idx == 0), -jnp.inf, 0.0)

    lane = lax.broadcasted_iota(jnp.int32, (1, KVD), 1)
    low = lane < HD
    high = jnp.logical_not(low)
    k_rot = pltpu.roll(k_all, HD, 1)
    v_rot = pltpu.roll(v_all, HD, 1)

    def padded(t_all, t_rot, kvh, parity):
        src = t_all if kvh == parity else t_rot
        return jnp.where(low if parity == 0 else high, src, 0.0).astype(BF16)

    ones = [jnp.broadcast_to(jnp.where(sel, 1.0, 0.0), (2 * W, KVD)).astype(BF16) for sel in (low, high)]

    for kvh in range(ATT_KV_HEADS):
        k_pad = [padded(k_all, k_rot, kvh, parity) for parity in range(2)]
        v_pad = [padded(v_all, v_rot, kvh, parity) for parity in range(2)]
        for n in range(T // W):
            rows = slice(n * W, (n + 1) * W)
            band = slice(n * W, (n + 2) * W)
            qs = jnp.concatenate(
                [q[rows, (kvh * pairs + m) * LANES:(kvh * pairs + m + 1) * LANES] for m in range(pairs)],
                axis=0)
            probs, sink_terms = [], []
            for parity in range(2):
                sc = _dot_nt(qs, k_pad[parity][band])
                f = jnp.where(upper, sc[:, :W], sc[:, W:]) + bias_s[2 * kvh + parity]
                if n == 0:
                    f = f + first_mask
                mx = jnp.broadcast_to(jnp.max(f, axis=-1, keepdims=True), (R, W))
                e = jnp.exp(f - mx)
                sink_terms.append(jnp.exp(sinkb_s[2 * kvh + parity] - mx))
                probs.append(jnp.concatenate([jnp.where(upper, e, 0.0), jnp.where(upper, 0.0, e)],
                                             axis=1).astype(BF16))
            lhs = jnp.concatenate(probs, axis=1)
            rhs = jnp.concatenate(
                [jnp.concatenate([v_pad[parity][band], ones[parity]], axis=1) for parity in range(2)],
                axis=0)
            pv = _dot(lhs, rhs)
            out = pv[:, :KVD] / (pv[:, KVD:] + jnp.where(low, sink_terms[0], sink_terms[1]))
            for m in range(pairs):
                cols = slice((kvh * pairs + m) * LANES, (kvh * pairs + m + 1) * LANES)
                attn_s[rows, cols] = out[m * W:(m + 1) * W]

    out_ref[0] = h + _dot(attn_s[...].astype(BF16), wo_ref[...])


def _attn_layer(h, attn_g, kv_g, w_q, w_kv, sinks, w_o, layer):
    B, S, D = h.shape
    T = min(ATT_TILE, S)
    assert S % T == 0 and T % WINDOW == 0
    assert w_q.shape[2] == ATT_Q_HEADS * ATT_HEAD_DIM and w_kv.shape[1] == 2 * ATT_KV_HEADS * ATT_HEAD_DIM
    tile = pl.BlockSpec((1, T, D), lambda b, s: (b, s, 0))
    kvd = ATT_KV_HEADS * ATT_HEAD_DIM
    return pl.pallas_call(
        functools.partial(_attn_kernel, layer=layer),
        grid=(B, S // T),
        in_specs=[pl.BlockSpec(memory_space=pltpu.SMEM), tile, _resident(attn_g.shape), _resident((1, D)),
                  _resident_layer(w_q.shape, layer), _resident(w_kv.shape), _resident_layer(w_o.shape, layer)],
        out_specs=tile,
        out_shape=jax.ShapeDtypeStruct(h.shape, F32),
        scratch_shapes=[pltpu.VMEM((WINDOW, kvd), F32), pltpu.VMEM((WINDOW, kvd), F32),
                        pltpu.VMEM((2 * ATT_KV_HEADS, ATT_GROUP // 2 * WINDOW, WINDOW), F32),
                        pltpu.VMEM((2 * ATT_KV_HEADS, ATT_GROUP // 2 * WINDOW, WINDOW), F32),
                        pltpu.VMEM((T, ATT_Q_HEADS * ATT_HEAD_DIM), F32)],
        compiler_params=pltpu.CompilerParams(dimension_semantics=("arbitrary", "arbitrary"),
                                             vmem_limit_bytes=VMEM_LIMIT),
        name="swa_layer",
    )(sinks, h, attn_g, kv_g.reshape(1, D), w_q, w_kv, w_o)


def kernel(x, hg_norm, hg_w_in, hg_lb_logits, hg_out_norm, hg_w_out, kv_norm, w_kv, attn_norm, attn_w_q,
           attn_sinks, attn_w_o, ffn_norm, ffn_w_up, ffn_conv_w, ffn_conv_b, ffn_w_down, final_norm):
    depth = ffn_norm.shape[0]
    n_a = hg_norm.shape[0]
    assert depth - n_a == 1
    h = x
    for layer in range(depth):
        if layer < n_a:
            h = _hgrn_layer(h, hg_norm, hg_lb_logits, hg_w_in, hg_out_norm, hg_w_out, layer)
        else:
            h = _attn_layer(h, attn_norm, kv_norm, attn_w_q, w_kv, attn_sinks, attn_w_o, layer - n_a)
        h = _ffn_layer(h, ffn_norm, ffn_w_up, ffn_conv_w, ffn_conv_b, ffn_w_down, final_norm, layer,
                       layer == depth - 1)
    return h
```

```python
import functools

import jax
import jax.numpy as jnp
from jax import lax
from jax.experimental import pallas as pl
from jax.experimental.pallas import tpu as pltpu

F32 = jnp.float32
BF16 = jnp.bfloat16
EPS = 1e-6

HG_HEADS = 8
HG_DK = 128
ATT_HEAD_DIM = 64
ATT_Q_HEADS = 16
ATT_KV_HEADS = 2
ATT_GROUP = ATT_Q_HEADS // ATT_KV_HEADS
WINDOW = 128
CONV_WIDTH = 3

HG_TILE = 512
HG_CHUNK = 128
FFN_TILE = 1024
FFN_FCHUNK = 256
ATT_TILE = 1024
SAFE_LOG_DECAY = -80.0
VMEM_LIMIT = 56 * 1024 * 1024
SUBLANES = 8
LANES = 128


def _dot(a, b):
    return lax.dot_general(a, b, (((1,), (0,)), ((), ())), preferred_element_type=F32)


def _dot_nt(a, b):
    return lax.dot_general(a, b, (((1,), (1,)), ((), ())), preferred_element_type=F32)


def _rms(x, g):
    ms = jnp.mean(x * x, axis=-1, keepdims=True)
    return x * lax.rsqrt(ms + EPS) * g


def _sigmoid(x):
    return 1.0 / (1.0 + jnp.exp(-x))


def _resident(shape):
    nd = len(shape)
    return pl.BlockSpec(shape, lambda *_: (0,) * nd, pipeline_mode=pl.Buffered(1))


def _resident_layer(shape, layer):
    nd = len(shape)
    return pl.BlockSpec((None,) + tuple(shape[1:]), lambda *_: (layer,) + (0,) * (nd - 1),
                        pipeline_mode=pl.Buffered(1))


def _hgrn_kernel(x_ref, g_ref, lbl_ref, win_ref, onorm_ref, wout_ref, out_ref,
                 st_ref, stn_ref, q_s, k_s, v_s, cum_s, oi_s, o_s, qt_s, a_s, vt_s, inc_s, *, layer):
    T, D = q_s.shape
    C = HG_CHUNK
    n_chunks = T // C
    s_idx = pl.program_id(1)

    @pl.when(s_idx == 0)
    def _():
        st_ref[...] = jnp.zeros_like(st_ref)

    x = x_ref[0]
    xn = _rms(x, g_ref[layer:layer + 1, :]).astype(BF16)

    lg = lbl_ref[...]
    le = jnp.exp(lg - jnp.max(lg, axis=0, keepdims=True))
    lb = jnp.sum(le[0:layer + 1], axis=0, keepdims=True) / jnp.sum(le, axis=0, keepdims=True)

    pq = _dot(xn, win_ref[:, 0:D])
    q_s[...] = pq * _sigmoid(pq) * (HG_DK ** -0.5)
    pf = _dot(xn, win_ref[:, D:2 * D])
    forget = lb + (1.0 - lb) * _sigmoid(pf)
    k_s[...] = 1.0 - forget
    logf = jnp.log(forget)
    v_s[...] = _dot(xn, win_ref[:, 2 * D:3 * D])
    pg = _dot(xn, win_ref[:, 3 * D:4 * D])
    gate = pg * _sigmoid(pg)

    row = lax.broadcasted_iota(jnp.int32, (C, C), 0)
    col = lax.broadcasted_iota(jnp.int32, (C, C), 1)
    causal = row >= col
    tri = jnp.where(causal, 1.0, 0.0).astype(BF16)
    l_hi = logf.astype(BF16)
    l_lo = (logf - l_hi.astype(F32)).astype(BF16)
    tri2 = jnp.concatenate([tri, tri], axis=1)
    for c in range(n_chunks):
        r = slice(c * C, (c + 1) * C)
        cum_s[r, :] = _dot(tri2, jnp.concatenate([l_hi[r], l_lo[r]], axis=0))

    lasts = jnp.concatenate([cum_s[(c + 1) * C - 1:(c + 1) * C, :] for c in range(n_chunks)], axis=0)
    safe = jnp.min(lasts) > SAFE_LOG_DECAY

    def stage(fast):
        for c in range(n_chunks):
            r = slice(c * C, (c + 1) * C)
            for h in range(HG_HEADS):
                hs = slice(h * HG_DK, (h + 1) * HG_DK)
                cum = cum_s[r, hs]
                last = cum[C - 1:C, :]
                qt = (q_s[r, hs] * jnp.exp(cum)).astype(BF16)
                vt = v_s[r, hs].T.astype(BF16)
                qt_s[r, hs] = qt
                vt_s[c * HG_HEADS + h] = vt
                if fast:
                    kt = k_s[r, hs] * jnp.exp(-cum)
                    a_s[r, hs] = jnp.where(causal, _dot_nt(qt, kt.astype(BF16)), 0.0).astype(BF16)
                    kd = (kt * jnp.exp(last)).astype(BF16)
                else:
                    kd = (k_s[r, hs] * jnp.exp(last - cum)).astype(BF16)
                inc_s[c * HG_HEADS + h] = _dot(vt, kd)

    def scan(fast):
        states = [st_ref[h] for h in range(HG_HEADS)]
        for c in range(n_chunks):
            r = slice(c * C, (c + 1) * C)
            for h in range(HG_HEADS):
                hs = slice(h * HG_DK, (h + 1) * HG_DK)
                st_b = states[h].astype(BF16)
                if fast:
                    o_s[r, hs] = _dot_nt(jnp.concatenate([a_s[r, hs], qt_s[r, hs]], axis=1),
                                         jnp.concatenate([vt_s[c * HG_HEADS + h], st_b], axis=1))
                else:
                    o_s[r, hs] = oi_s[r, hs] + _dot_nt(qt_s[r, hs], st_b)
                decay = jnp.exp(cum_s[(c + 1) * C - 1:(c + 1) * C, hs])
                states[h] = states[h] * decay + inc_s[c * HG_HEADS + h]
        for h in range(HG_HEADS):
            stn_ref[h] = states[h]

    def finish():
        onorm = onorm_ref[layer:layer + 1, :]
        parts = []
        for h in range(HG_HEADS):
            hs = slice(h * HG_DK, (h + 1) * HG_DK)
            parts.append(_rms(o_s[:, hs], onorm) * gate[:, hs])
        o = jnp.concatenate(parts, axis=-1).astype(BF16)
        out_ref[0] = x + _dot(o, wout_ref[...])

    stage(True)
    scan(True)
    finish()

    @pl.when(jnp.logical_not(safe))
    def _():
        tcol = lax.broadcasted_iota(jnp.int32, (C, 1), 0)
        for c in range(n_chunks):
            r = slice(c * C, (c + 1) * C)
            qc = q_s[r, :]
            cumc = cum_s[r, :]

            def body(j, acc):
                kj = k_s[pl.ds(c * C + j, 1), :]
                vj = v_s[pl.ds(c * C + j, 1), :]
                cj = cum_s[pl.ds(c * C + j, 1), :]
                w = qc * kj * jnp.exp(jnp.minimum(cumc - cj, 0.0))
                keep = tcol >= j
                parts = []
                for h in range(HG_HEADS):
                    hs = slice(h * HG_DK, (h + 1) * HG_DK)
                    sc = jnp.sum(w[:, hs], axis=-1, keepdims=True)
                    parts.append(jnp.where(keep, sc, 0.0) * vj[:, hs])
                return acc + jnp.concatenate(parts, axis=-1)

            oi_s[r, :] = lax.fori_loop(0, C, body, jnp.zeros((C, D), F32))
        stage(False)
        scan(False)
        finish()

    st_ref[...] = stn_ref[...]


def _hgrn_layer(x, norm_g, lb_logits, w_in, out_norm, w_out, layer):
    B, S, D = x.shape
    T = min(HG_TILE, S)
    assert S % T == 0 and T % HG_CHUNK == 0 and D == HG_HEADS * HG_DK
    tile = pl.BlockSpec((1, T, D), lambda b, s: (b, s, 0))
    return pl.pallas_call(
        functools.partial(_hgrn_kernel, layer=layer),
        grid=(B, S // T),
        in_specs=[tile, _resident(norm_g.shape), _resident(lb_logits.shape), _resident_layer(w_in.shape, layer),
                  _resident(out_norm.shape), _resident_layer(w_out.shape, layer)],
        out_specs=tile,
        out_shape=jax.ShapeDtypeStruct(x.shape, F32),
        scratch_shapes=([pltpu.VMEM((HG_HEADS, HG_DK, HG_DK), F32)] * 2 + [pltpu.VMEM((T, D), F32)] * 6
                        + [pltpu.VMEM((T, D), BF16)] * 2
                        + [pltpu.VMEM((T // HG_CHUNK * HG_HEADS, HG_DK, HG_CHUNK), BF16),
                           pltpu.VMEM((T // HG_CHUNK * HG_HEADS, HG_DK, HG_DK), F32)]),
        compiler_params=pltpu.CompilerParams(dimension_semantics=("arbitrary", "arbitrary"),
                                             vmem_limit_bytes=VMEM_LIMIT),
        name="hgrn2_layer",
    )(x, norm_g, lb_logits, w_in, out_norm, w_out)


def _ffn_kernel(x_ref, g_ref, wup_ref, cw_ref, cb_ref, wdn_ref, fg_ref, out_ref, carry_ref, act_ref,
                *, layer, final_norm):
    T, F = act_ref.shape
    FC = FFN_FCHUNK
    s_idx = pl.program_id(1)
    x = x_ref[0]
    xn = _rms(x, g_ref[layer:layer + 1, :]).astype(BF16)
    has_prev = s_idx > 0
    rid = lax.broadcasted_iota(jnp.int32, (SUBLANES, 1), 0)
    for j in range(F // FC):
        cs = slice(j * FC, (j + 1) * FC)
        gate = _dot(xn, wup_ref[:, cs])
        val = _dot(xn, wup_ref[:, F + j * FC:F + (j + 1) * FC])
        prev = jnp.where(has_prev, carry_ref[:, cs], 0.0)
        carry_ref[:, cs] = gate[T - SUBLANES:, :]
        p1 = prev[SUBLANES - 1:SUBLANES, :]
        p2 = prev[SUBLANES - 2:SUBLANES - 1, :]
        g1 = pltpu.roll(gate, 1, 0)
        g2 = pltpu.roll(gate, 2, 0)
        g1 = jnp.concatenate([jnp.where(rid == 0, p1, g1[:SUBLANES]), g1[SUBLANES:]], axis=0)
        g2 = jnp.concatenate([jnp.where(rid == 0, p2, jnp.where(rid == 1, p1, g2[:SUBLANES])), g2[SUBLANES:]],
                             axis=0)
        conv = cb_ref[layer:layer + 1, cs] + cw_ref[0:1, cs] * g2 + cw_ref[1:2, cs] * g1 + cw_ref[2:3, cs] * gate
        act_ref[:, cs] = (conv * _sigmoid(conv) * val).astype(BF16)
    h = x + _dot(act_ref[...], wdn_ref[...])
    if final_norm:
        h = _rms(h, fg_ref[...])
    out_ref[0] = h


def _ffn_layer(x, norm_g, w_up, conv_w, conv_b, w_down, final_g, layer, final_norm):
    B, S, D = x.shape
    F = w_down.shape[1]
    T = min(FFN_TILE, S)
    assert S % T == 0 and F % FFN_FCHUNK == 0 and conv_w.shape[1] == CONV_WIDTH
    tile = pl.BlockSpec((1, T, D), lambda b, s: (b, s, 0))
    return pl.pallas_call(
        functools.partial(_ffn_kernel, layer=layer, final_norm=final_norm),
        grid=(B, S // T),
        in_specs=[tile, _resident(norm_g.shape), _resident_layer(w_up.shape, layer),
                  _resident_layer(conv_w.shape, layer), _resident(conv_b.shape),
                  _resident_layer(w_down.shape, layer), _resident((1, D))],
        out_specs=tile,
        out_shape=jax.ShapeDtypeStruct(x.shape, F32),
        scratch_shapes=[pltpu.VMEM((SUBLANES, F), F32), pltpu.VMEM((T, F), BF16)],
        compiler_params=pltpu.CompilerParams(dimension_semantics=("arbitrary", "arbitrary"),
                                             vmem_limit_bytes=VMEM_LIMIT),
        name="conv_ffn_final" if final_norm else "conv_ffn",
    )(x, norm_g, w_up, conv_w, conv_b, w_down, final_g.reshape(1, D))


def _attn_kernel(sink_ref, h_ref, ag_ref, kg_ref, wq_ref, wkv_ref, wo_ref, out_ref,
                 kprev_ref, vprev_ref, bias_s, sinkb_s, attn_s, *, layer):
    T, D = attn_s.shape
    W = WINDOW
    HD = ATT_HEAD_DIM
    KVD = ATT_KV_HEADS * HD
    pairs = ATT_GROUP // 2
    s_idx = pl.program_id(1)

    @pl.when(s_idx == 0)
    def _():
        kprev_ref[...] = jnp.zeros_like(kprev_ref)
        vprev_ref[...] = jnp.zeros_like(vprev_ref)

    h = h_ref[0]
    hn = h * lax.rsqrt(jnp.mean(h * h, axis=-1, keepdims=True) + EPS)
    q = (_dot((hn * ag_ref[layer:layer + 1, :]).astype(BF16), wq_ref[...]) * (HD ** -0.5)).astype(BF16)
    kv = _dot((hn * kg_ref[...]).astype(BF16), wkv_ref[...])
    k_all = jnp.concatenate([kprev_ref[...], kv[:, :KVD]], axis=0)
    v_all = jnp.concatenate([vprev_ref[...], kv[:, KVD:]], axis=0)
    kprev_ref[...] = kv[T - W:, :KVD]
    vprev_ref[...] = kv[T - W:, KVD:]

    R = pairs * W
    ii = lax.broadcasted_iota(jnp.int32, (R, W), 0) & (W - 1)
    jj = lax.broadcasted_iota(jnp.int32, (R, W), 1)
    upper = jj > ii
    dist = jnp.where(upper, ii - jj + W, ii - jj).astype(F32)
    piece = lax.broadcasted_iota(jnp.int32, (R, W), 0) // W
    for kvh in range(ATT_KV_HEADS):
        for parity in range(2):
            slope = jnp.zeros((R, W), F32)
            sink = jnp.zeros((R, W), F32)
            for m in range(pairs):
                hq = kvh * ATT_GROUP + 2 * m + parity
                slope = jnp.where(piece == m, 2.0 ** (-8.0 * (hq + 1) / ATT_Q_HEADS), slope)
                sink = jnp.where(piece == m, sink_ref[layer, hq], sink)
            bias_s[2 * kvh + parity] = -slope * dist
            sinkb_s[2 * kvh + parity] = sink
    first_mask = jnp.where(upper & (s_idx == 0), -jnp.inf, 0.0)

    lane = lax.broadcasted_iota(jnp.int32, (1, KVD), 1)
    low = lane < HD
    high = jnp.logical_not(low)
    k_rot = pltpu.roll(k_all, HD, 1)
    v_rot = pltpu.roll(v_all, HD, 1)

    def padded(t_all, t_rot, kvh, parity):
        src = t_all if kvh == parity else t_rot
        return jnp.where(low if parity == 0 else high, src, 0.0).astype(BF16)

    ones = [jnp.broadcast_to(jnp.where(sel, 1.0, 0.0), (2 * W, KVD)).astype(BF16) for sel in (low, high)]

    for kvh in range(ATT_KV_HEADS):
        k_pad = [padded(k_all, k_rot, kvh, parity) for parity in range(2)]
        v_pad = [padded(v_all, v_rot, kvh, parity) for parity in range(2)]
        for n in range(T // W):
            rows = slice(n * W, (n + 1) * W)
            band = slice(n * W, (n + 2) * W)
            qs = jnp.concatenate(
                [q[rows, (kvh * pairs + m) * LANES:(kvh * pairs + m + 1) * LANES] for m in range(pairs)],
                axis=0)
            probs, sink_terms = [], []
            for parity in range(2):
                sc = _dot_nt(qs, k_pad[parity][band])
                f = jnp.where(upper, sc[:, :W], sc[:, W:]) + bias_s[2 * kvh + parity]
                if n == 0:
                    f = f + first_mask
                mx = jnp.broadcast_to(jnp.max(f, axis=-1, keepdims=True), (R, W))
                e = jnp.exp(f - mx)
                sink_terms.append(jnp.exp(sinkb_s[2 * kvh + parity] - mx))
                probs.append(jnp.concatenate([jnp.where(upper, e, 0.0), jnp.where(upper, 0.0, e)],
                                             axis=1).astype(BF16))
            lhs = jnp.concatenate(probs, axis=1)
            rhs = jnp.concatenate(
                [jnp.concatenate([v_pad[parity][band], ones[parity]], axis=1) for parity in range(2)],
                axis=0)
            pv = _dot(lhs, rhs)
            out = pv[:, :KVD] / (pv[:, KVD:] + jnp.where(low, sink_terms[0], sink_terms[1]))
            for m in range(pairs):
                cols = slice((kvh * pairs + m) * LANES, (kvh * pairs + m + 1) * LANES)
                attn_s[rows, cols] = out[m * W:(m + 1) * W]

    out_ref[0] = h + _dot(attn_s[...].astype(BF16), wo_ref[...])


def _attn_layer(h, attn_g, kv_g, w_q, w_kv, sinks, w_o, layer):
    B, S, D = h.shape
    T = min(ATT_TILE, S)
    assert S % T == 0 and T % WINDOW == 0
    assert w_q.shape[2] == ATT_Q_HEADS * ATT_HEAD_DIM and w_kv.shape[1] == 2 * ATT_KV_HEADS * ATT_HEAD_DIM
    tile = pl.BlockSpec((1, T, D), lambda b, s: (b, s, 0))
    kvd = ATT_KV_HEADS * ATT_HEAD_DIM
    return pl.pallas_call(
        functools.partial(_attn_kernel, layer=layer),
        grid=(B, S // T),
        in_specs=[pl.BlockSpec(memory_space=pltpu.SMEM), tile, _resident(attn_g.shape), _resident((1, D)),
                  _resident_layer(w_q.shape, layer), _resident(w_kv.shape), _resident_layer(w_o.shape, layer)],
        out_specs=tile,
        out_shape=jax.ShapeDtypeStruct(h.shape, F32),
        scratch_shapes=[pltpu.VMEM((WINDOW, kvd), F32), pltpu.VMEM((WINDOW, kvd), F32),
                        pltpu.VMEM((2 * ATT_KV_HEADS, ATT_GROUP // 2 * WINDOW, WINDOW), F32),
                        pltpu.VMEM((2 * ATT_KV_HEADS, ATT_GROUP // 2 * WINDOW, WINDOW), F32),
                        pltpu.VMEM((T, ATT_Q_HEADS * ATT_HEAD_DIM), F32)],
        compiler_params=pltpu.CompilerParams(dimension_semantics=("arbitrary", "arbitrary"),
                                             vmem_limit_bytes=VMEM_LIMIT),
        name="swa_layer",
    )(sinks, h, attn_g, kv_g.reshape(1, D), w_q, w_kv, w_o)


def kernel(x, hg_norm, hg_w_in, hg_lb_logits, hg_out_norm, hg_w_out, kv_norm, w_kv, attn_norm, attn_w_q,
           attn_sinks, attn_w_o, ffn_norm, ffn_w_up, ffn_conv_w, ffn_conv_b, ffn_w_down, final_norm):
    depth = ffn_norm.shape[0]
    n_a = hg_norm.shape[0]
    assert depth - n_a == 1
    ffn_w_up = ffn_w_up.astype(BF16)
    ffn_w_down = ffn_w_down.astype(BF16)
    h = x
    for layer in range(depth):
        if layer < n_a:
            h = _hgrn_layer(h, hg_norm, hg_lb_logits, hg_w_in, hg_out_norm, hg_w_out, layer)
        else:
            h = _attn_layer(h, attn_norm, kv_norm, attn_w_q, w_kv, attn_sinks, attn_w_o, layer - n_a)
        h = _ffn_layer(h, ffn_norm, ffn_w_up, ffn_conv_w, ffn_conv_b, ffn_w_down, final_norm, layer,
                       layer == depth - 1)
    return h
```

```python
import functools

import jax
import jax.numpy as jnp
from jax import lax
from jax.experimental import pallas as pl
from jax.experimental.pallas import tpu as pltpu

F32 = jnp.float32
BF16 = jnp.bfloat16
EPS = 1e-6

HG_HEADS = 8
HG_DK = 128
ATT_HEAD_DIM = 64
ATT_Q_HEADS = 16
ATT_KV_HEADS = 2
ATT_GROUP = ATT_Q_HEADS // ATT_KV_HEADS
WINDOW = 128
CONV_WIDTH = 3

HG_TILE = 512
HG_CHUNK = 128
FFN_TILE = 1024
FFN_FCHUNK = 256
ATT_TILE = 1024
SAFE_LOG_DECAY = -80.0
VMEM_LIMIT = 56 * 1024 * 1024
SUBLANES = 8
LANES = 128


def _dot(a, b):
    return lax.dot_general(a, b, (((1,), (0,)), ((), ())), preferred_element_type=F32)


def _dot_nt(a, b):
    return lax.dot_general(a, b, (((1,), (1,)), ((), ())), preferred_element_type=F32)


def _rms(x, g):
    ms = jnp.mean(x * x, axis=-1, keepdims=True)
    return x * lax.rsqrt(ms + EPS) * g


def _sigmoid(x):
    return 1.0 / (1.0 + jnp.exp(-x))


def _resident(shape):
    nd = len(shape)
    return pl.BlockSpec(shape, lambda *_: (0,) * nd, pipeline_mode=pl.Buffered(1))


def _resident_layer(shape, layer):
    nd = len(shape)
    return pl.BlockSpec((None,) + tuple(shape[1:]), lambda *_: (layer,) + (0,) * (nd - 1),
                        pipeline_mode=pl.Buffered(1))


BF16_SUBLANES = 16


def _side_cast_specs(w, layer, n_seq_tiles, n_steps):
    _, rows, cols = w.shape
    rpb = next(r for r in range(BF16_SUBLANES, rows + 1, BF16_SUBLANES)
               if rows % r == 0 and rows // r <= n_steps)
    last = rows // rpb - 1

    def block(b, s):
        return jnp.minimum(b * n_seq_tiles + s, last)
    return (pl.BlockSpec((None, rpb, cols), lambda b, s: (layer, block(b, s), 0)),
            pl.BlockSpec((rpb, cols), lambda b, s: (block(b, s), 0)),
            jax.ShapeDtypeStruct((rows, cols), BF16))


def _hgrn_kernel(x_ref, g_ref, lbl_ref, win_ref, onorm_ref, wout_ref, cast_a_ref, cast_b_ref,
                 out_ref, cast_a_out, cast_b_out,
                 st_ref, stn_ref, q_s, k_s, v_s, cum_s, o_s, qt_s, a_s, vt_s, inc_s, *, layer):
    T, D = q_s.shape
    oi_s = o_s
    cast_a_out[...] = cast_a_ref[...].astype(BF16)
    cast_b_out[...] = cast_b_ref[...].astype(BF16)
    C = HG_CHUNK
    n_chunks = T // C
    s_idx = pl.program_id(1)

    @pl.when(s_idx == 0)
    def _():
        st_ref[...] = jnp.zeros_like(st_ref)

    x = x_ref[0]
    xn = _rms(x, g_ref[layer:layer + 1, :]).astype(BF16)

    lg = lbl_ref[...]
    le = jnp.exp(lg - jnp.max(lg, axis=0, keepdims=True))
    lb = jnp.sum(le[0:layer + 1], axis=0, keepdims=True) / jnp.sum(le, axis=0, keepdims=True)

    pq = _dot(xn, win_ref[:, 0:D])
    q_s[...] = pq * _sigmoid(pq) * (HG_DK ** -0.5)
    pf = _dot(xn, win_ref[:, D:2 * D])
    forget = lb + (1.0 - lb) * _sigmoid(pf)
    k_s[...] = 1.0 - forget
    logf = jnp.log(forget)
    v_s[...] = _dot(xn, win_ref[:, 2 * D:3 * D])
    pg = _dot(xn, win_ref[:, 3 * D:4 * D])
    gate = pg * _sigmoid(pg)

    row = lax.broadcasted_iota(jnp.int32, (C, C), 0)
    col = lax.broadcasted_iota(jnp.int32, (C, C), 1)
    causal = row >= col
    tri = jnp.where(causal, 1.0, 0.0).astype(BF16)
    l_hi = logf.astype(BF16)
    l_lo = (logf - l_hi.astype(F32)).astype(BF16)
    tri2 = jnp.concatenate([tri, tri], axis=1)
    for c in range(n_chunks):
        r = slice(c * C, (c + 1) * C)
        cum_s[r, :] = _dot(tri2, jnp.concatenate([l_hi[r], l_lo[r]], axis=0))

    lasts = jnp.concatenate([cum_s[(c + 1) * C - 1:(c + 1) * C, :] for c in range(n_chunks)], axis=0)
    safe = jnp.min(lasts) > SAFE_LOG_DECAY

    def stage(fast):
        for c in range(n_chunks):
            r = slice(c * C, (c + 1) * C)
            for h in range(HG_HEADS):
                hs = slice(h * HG_DK, (h + 1) * HG_DK)
                cum = cum_s[r, hs]
                last = cum[C - 1:C, :]
                qt = (q_s[r, hs] * jnp.exp(cum)).astype(BF16)
                vt = v_s[r, hs].T.astype(BF16)
                qt_s[r, hs] = qt
                vt_s[c * HG_HEADS + h] = vt
                if fast:
                    kt = k_s[r, hs] * jnp.exp(-cum)
                    a_s[r, hs] = jnp.where(causal, _dot_nt(qt, kt.astype(BF16)), 0.0).astype(BF16)
                    kd = (kt * jnp.exp(last)).astype(BF16)
                else:
                    kd = (k_s[r, hs] * jnp.exp(last - cum)).astype(BF16)
                inc_s[c * HG_HEADS + h] = _dot(vt, kd)

    def scan(fast):
        states = [st_ref[h] for h in range(HG_HEADS)]
        for c in range(n_chunks):
            r = slice(c * C, (c + 1) * C)
            for h in range(HG_HEADS):
                hs = slice(h * HG_DK, (h + 1) * HG_DK)
                st_b = states[h].astype(BF16)
                if fast:
                    o_s[r, hs] = _dot_nt(jnp.concatenate([a_s[r, hs], qt_s[r, hs]], axis=1),
                                         jnp.concatenate([vt_s[c * HG_HEADS + h], st_b], axis=1))
                else:
                    o_s[r, hs] = oi_s[r, hs] + _dot_nt(qt_s[r, hs], st_b)
                decay = jnp.exp(cum_s[(c + 1) * C - 1:(c + 1) * C, hs])
                states[h] = states[h] * decay + inc_s[c * HG_HEADS + h]
        for h in range(HG_HEADS):
            stn_ref[h] = states[h]

    def finish():
        onorm = onorm_ref[layer:layer + 1, :]
        parts = []
        for h in range(HG_HEADS):
            hs = slice(h * HG_DK, (h + 1) * HG_DK)
            parts.append(_rms(o_s[:, hs], onorm) * gate[:, hs])
        o = jnp.concatenate(parts, axis=-1).astype(BF16)
        out_ref[0] = x + _dot(o, wout_ref[...])

    stage(True)
    scan(True)
    finish()

    @pl.when(jnp.logical_not(safe))
    def _():
        tcol = lax.broadcasted_iota(jnp.int32, (C, 1), 0)
        for c in range(n_chunks):
            r = slice(c * C, (c + 1) * C)
            qc = q_s[r, :]
            cumc = cum_s[r, :]

            def body(j, acc):
                kj = k_s[pl.ds(c * C + j, 1), :]
                vj = v_s[pl.ds(c * C + j, 1), :]
                cj = cum_s[pl.ds(c * C + j, 1), :]
                w = qc * kj * jnp.exp(jnp.minimum(cumc - cj, 0.0))
                keep = tcol >= j
                parts = []
                for h in range(HG_HEADS):
                    hs = slice(h * HG_DK, (h + 1) * HG_DK)
                    sc = jnp.sum(w[:, hs], axis=-1, keepdims=True)
                    parts.append(jnp.where(keep, sc, 0.0) * vj[:, hs])
                return acc + jnp.concatenate(parts, axis=-1)

            oi_s[r, :] = lax.fori_loop(0, C, body, jnp.zeros((C, D), F32))
        stage(False)
        scan(False)
        finish()

    st_ref[...] = stn_ref[...]


def _hgrn_layer(x, norm_g, lb_logits, w_in, out_norm, w_out, layer, cast_a, cast_b, cast_layer):
    B, S, D = x.shape
    T = min(HG_TILE, S)
    assert S % T == 0 and T % HG_CHUNK == 0 and D == HG_HEADS * HG_DK
    tile = pl.BlockSpec((1, T, D), lambda b, s: (b, s, 0))
    casts = [_side_cast_specs(w, cast_layer, S // T, B * (S // T)) for w in (cast_a, cast_b)]
    return pl.pallas_call(
        functools.partial(_hgrn_kernel, layer=layer),
        grid=(B, S // T),
        in_specs=[tile, _resident(norm_g.shape), _resident(lb_logits.shape), _resident_layer(w_in.shape, layer),
                  _resident(out_norm.shape), _resident_layer(w_out.shape, layer), casts[0][0], casts[1][0]],
        out_specs=(tile, casts[0][1], casts[1][1]),
        out_shape=(jax.ShapeDtypeStruct(x.shape, F32), casts[0][2], casts[1][2]),
        scratch_shapes=([pltpu.VMEM((HG_HEADS, HG_DK, HG_DK), F32)] * 2 + [pltpu.VMEM((T, D), F32)] * 5
                        + [pltpu.VMEM((T, D), BF16)] * 2
                        + [pltpu.VMEM((T // HG_CHUNK * HG_HEADS, HG_DK, HG_CHUNK), BF16),
                           pltpu.VMEM((T // HG_CHUNK * HG_HEADS, HG_DK, HG_DK), F32)]),
        compiler_params=pltpu.CompilerParams(dimension_semantics=("arbitrary", "arbitrary"),
                                             vmem_limit_bytes=VMEM_LIMIT),
        name="hgrn2_layer",
    )(x, norm_g, lb_logits, w_in, out_norm, w_out, cast_a, cast_b)


def _ffn_kernel(x_ref, g_ref, wup_ref, cw_ref, cb_ref, wdn_ref, fg_ref, out_ref, carry_ref, act_ref,
                *, layer, final_norm):
    T, F = act_ref.shape
    FC = FFN_FCHUNK
    s_idx = pl.program_id(1)
    x = x_ref[0]
    xn = _rms(x, g_ref[layer:layer + 1, :]).astype(BF16)
    has_prev = s_idx > 0
    rid = lax.broadcasted_iota(jnp.int32, (SUBLANES, 1), 0)
    for j in range(F // FC):
        cs = slice(j * FC, (j + 1) * FC)
        gate = _dot(xn, wup_ref[:, cs])
        val = _dot(xn, wup_ref[:, F + j * FC:F + (j + 1) * FC])
        prev = jnp.where(has_prev, carry_ref[:, cs], 0.0)
        carry_ref[:, cs] = gate[T - SUBLANES:, :]
        p1 = prev[SUBLANES - 1:SUBLANES, :]
        p2 = prev[SUBLANES - 2:SUBLANES - 1, :]
        g1 = pltpu.roll(gate, 1, 0)
        g2 = pltpu.roll(gate, 2, 0)
        g1 = jnp.concatenate([jnp.where(rid == 0, p1, g1[:SUBLANES]), g1[SUBLANES:]], axis=0)
        g2 = jnp.concatenate([jnp.where(rid == 0, p2, jnp.where(rid == 1, p1, g2[:SUBLANES])), g2[SUBLANES:]],
                             axis=0)
        conv = cb_ref[layer:layer + 1, cs] + cw_ref[0:1, cs] * g2 + cw_ref[1:2, cs] * g1 + cw_ref[2:3, cs] * gate
        act_ref[:, cs] = (conv * _sigmoid(conv) * val).astype(BF16)
    h = x + _dot(act_ref[...], wdn_ref[...])
    if final_norm:
        h = _rms(h, fg_ref[...])
    out_ref[0] = h


def _ffn_layer(x, norm_g, w_up, conv_w, conv_b, w_down, final_g, layer, final_norm):
    B, S, D = x.shape
    F = w_down.shape[0]
    T = min(FFN_TILE, S)
    assert S % T == 0 and F % FFN_FCHUNK == 0 and conv_w.shape[1] == CONV_WIDTH
    tile = pl.BlockSpec((1, T, D), lambda b, s: (b, s, 0))
    return pl.pallas_call(
        functools.partial(_ffn_kernel, layer=layer, final_norm=final_norm),
        grid=(B, S // T),
        in_specs=[tile, _resident(norm_g.shape), _resident(w_up.shape),
                  _resident_layer(conv_w.shape, layer), _resident(conv_b.shape),
                  _resident(w_down.shape), _resident((1, D))],
        out_specs=tile,
        out_shape=jax.ShapeDtypeStruct(x.shape, F32),
        scratch_shapes=[pltpu.VMEM((SUBLANES, F), F32), pltpu.VMEM((T, F), BF16)],
        compiler_params=pltpu.CompilerParams(dimension_semantics=("arbitrary", "arbitrary"),
                                             vmem_limit_bytes=VMEM_LIMIT),
        name="conv_ffn_final" if final_norm else "conv_ffn",
    )(x, norm_g, w_up, conv_w, conv_b, w_down, final_g.reshape(1, D))


def _attn_kernel(sink_ref, h_ref, ag_ref, kg_ref, wq_ref, wkv_ref, wo_ref, cast_a_ref, cast_b_ref,
                 out_ref, cast_a_out, cast_b_out,
                 kprev_ref, vprev_ref, bias_s, sinkb_s, attn_s, *, layer):
    T, D = attn_s.shape
    cast_a_out[...] = cast_a_ref[...].astype(BF16)
    cast_b_out[...] = cast_b_ref[...].astype(BF16)
    W = WINDOW
    HD = ATT_HEAD_DIM
    KVD = ATT_KV_HEADS * HD
    pairs = ATT_GROUP // 2
    s_idx = pl.program_id(1)

    @pl.when(s_idx == 0)
    def _():
        kprev_ref[...] = jnp.zeros_like(kprev_ref)
        vprev_ref[...] = jnp.zeros_like(vprev_ref)

    h = h_ref[0]
    hn = h * lax.rsqrt(jnp.mean(h * h, axis=-1, keepdims=True) + EPS)
    q = (_dot((hn * ag_ref[layer:layer + 1, :]).astype(BF16), wq_ref[...]) * (HD ** -0.5)).astype(BF16)
    kv = _dot((hn * kg_ref[...]).astype(BF16), wkv_ref[...])
    k_all = jnp.concatenate([kprev_ref[...], kv[:, :KVD]], axis=0)
    v_all = jnp.concatenate([vprev_ref[...], kv[:, KVD:]], axis=0)
    kprev_ref[...] = kv[T - W:, :KVD]
    vprev_ref[...] = kv[T - W:, KVD:]

    R = pairs * W
    ii = lax.broadcasted_iota(jnp.int32, (R, W), 0) & (W - 1)
    jj = lax.broadcasted_iota(jnp.int32, (R, W), 1)
    upper = jj > ii
    dist = jnp.where(upper, ii - jj + W, ii - jj).astype(F32)
    piece = lax.broadcasted_iota(jnp.int32, (R, W), 0) // W
    for kvh in range(ATT_KV_HEADS):
        for parity in range(2):
            slope = jnp.zeros((R, W), F32)
            sink = jnp.zeros((R, W), F32)
            for m in range(pairs):
                hq = kvh * ATT_GROUP + 2 * m + parity
                slope = jnp.where(piece == m, 2.0 ** (-8.0 * (hq + 1) / ATT_Q_HEADS), slope)
                sink = jnp.where(piece == m, sink_ref[layer, hq], sink)
            bias_s[2 * kvh + parity] = -slope * dist
            sinkb_s[2 * kvh + parity] = sink
    first_mask = jnp.where(upper & (s_idx == 0), -jnp.inf, 0.0)

    lane = lax.broadcasted_iota(jnp.int32, (1, KVD), 1)
    low = lane < HD
    high = jnp.logical_not(low)
    k_rot = pltpu.roll(k_all, HD, 1)
    v_rot = pltpu.roll(v_all, HD, 1)

    def padded(t_all, t_rot, kvh, parity):
        src = t_all if kvh == parity else t_rot
        return jnp.where(low if parity == 0 else high, src, 0.0).astype(BF16)

    ones = [jnp.broadcast_to(jnp.where(sel, 1.0, 0.0), (2 * W, KVD)).astype(BF16) for sel in (low, high)]

    for kvh in range(ATT_KV_HEADS):
        k_pad = [padded(k_all, k_rot, kvh, parity) for parity in range(2)]
        v_pad = [padded(v_all, v_rot, kvh, parity) for parity in range(2)]
        for n in range(T // W):
            rows = slice(n * W, (n + 1) * W)
            band = slice(n * W, (n + 2) * W)
            qs = jnp.concatenate(
                [q[rows, (kvh * pairs + m) * LANES:(kvh * pairs + m + 1) * LANES] for m in range(pairs)],
                axis=0)
            probs, sink_terms = [], []
            for parity in range(2):
                sc = _dot_nt(qs, k_pad[parity][band])
                f = jnp.where(upper, sc[:, :W], sc[:, W:]) + bias_s[2 * kvh + parity]
                if n == 0:
                    f = f + first_mask
                mx = jnp.broadcast_to(jnp.max(f, axis=-1, keepdims=True), (R, W))
                e = jnp.exp(f - mx)
                sink_terms.append(jnp.exp(sinkb_s[2 * kvh + parity] - mx))
                probs.append(jnp.concatenate([jnp.where(upper, e, 0.0), jnp.where(upper, 0.0, e)],
                                             axis=1).astype(BF16))
            lhs = jnp.concatenate(probs, axis=1)
            rhs = jnp.concatenate(
                [jnp.concatenate([v_pad[parity][band], ones[parity]], axis=1) for parity in range(2)],
                axis=0)
            pv = _dot(lhs, rhs)
            out = pv[:, :KVD] / (pv[:, KVD:] + jnp.where(low, sink_terms[0], sink_terms[1]))
            for m in range(pairs):
                cols = slice((kvh * pairs + m) * LANES, (kvh * pairs + m + 1) * LANES)
                attn_s[rows, cols] = out[m * W:(m + 1) * W]

    out_ref[0] = h + _dot(attn_s[...].astype(BF16), wo_ref[...])


def _attn_layer(h, attn_g, kv_g, w_q, w_kv, sinks, w_o, layer, cast_a, cast_b, cast_layer):
    B, S, D = h.shape
    T = min(ATT_TILE, S)
    assert S % T == 0 and T % WINDOW == 0
    assert w_q.shape[2] == ATT_Q_HEADS * ATT_HEAD_DIM and w_kv.shape[1] == 2 * ATT_KV_HEADS * ATT_HEAD_DIM
    tile = pl.BlockSpec((1, T, D), lambda b, s: (b, s, 0))
    kvd = ATT_KV_HEADS * ATT_HEAD_DIM
    casts = [_side_cast_specs(w, cast_layer, S // T, B * (S // T)) for w in (cast_a, cast_b)]
    return pl.pallas_call(
        functools.partial(_attn_kernel, layer=layer),
        grid=(B, S // T),
        in_specs=[pl.BlockSpec(memory_space=pltpu.SMEM), tile, _resident(attn_g.shape), _resident((1, D)),
                  _resident_layer(w_q.shape, layer), _resident(w_kv.shape), _resident_layer(w_o.shape, layer),
                  casts[0][0], casts[1][0]],
        out_specs=(tile, casts[0][1], casts[1][1]),
        out_shape=(jax.ShapeDtypeStruct(h.shape, F32), casts[0][2], casts[1][2]),
        scratch_shapes=[pltpu.VMEM((WINDOW, kvd), F32), pltpu.VMEM((WINDOW, kvd), F32),
                        pltpu.VMEM((2 * ATT_KV_HEADS, ATT_GROUP // 2 * WINDOW, WINDOW), F32),
                        pltpu.VMEM((2 * ATT_KV_HEADS, ATT_GROUP // 2 * WINDOW, WINDOW), F32),
                        pltpu.VMEM((T, ATT_Q_HEADS * ATT_HEAD_DIM), F32)],
        compiler_params=pltpu.CompilerParams(dimension_semantics=("arbitrary", "arbitrary"),
                                             vmem_limit_bytes=VMEM_LIMIT),
        name="swa_layer",
    )(sinks, h, attn_g, kv_g.reshape(1, D), w_q, w_kv, w_o, cast_a, cast_b)


def kernel(x, hg_norm, hg_w_in, hg_lb_logits, hg_out_norm, hg_w_out, kv_norm, w_kv, attn_norm, attn_w_q,
           attn_sinks, attn_w_o, ffn_norm, ffn_w_up, ffn_conv_w, ffn_conv_b, ffn_w_down, final_norm):
    depth = ffn_norm.shape[0]
    n_a = hg_norm.shape[0]
    assert depth - n_a == 1
    h = x
    for layer in range(depth):
        if layer < n_a:
            h, w_up, w_down = _hgrn_layer(h, hg_norm, hg_lb_logits, hg_w_in, hg_out_norm, hg_w_out, layer,
                                          ffn_w_up, ffn_w_down, layer)
        else:
            h, w_up, w_down = _attn_layer(h, attn_norm, kv_norm, attn_w_q, w_kv, attn_sinks, attn_w_o,
                                          layer - n_a, ffn_w_up, ffn_w_down, layer)
        h = _ffn_layer(h, ffn_norm, w_up, ffn_conv_w, ffn_conv_b, w_down, final_norm, layer,
                       layer == depth - 1)
    return h
```

```python
import functools

import jax
import jax.numpy as jnp
from jax import lax
from jax.experimental import pallas as pl
from jax.experimental.pallas import tpu as pltpu

F32 = jnp.float32
BF16 = jnp.bfloat16
EPS = 1e-6

HG_HEADS = 8
HG_DK = 128
ATT_HEAD_DIM = 64
ATT_Q_HEADS = 16
ATT_KV_HEADS = 2
ATT_GROUP = ATT_Q_HEADS // ATT_KV_HEADS
WINDOW = 128
CONV_WIDTH = 3

HG_TILE = 512
HG_CHUNK = 128
HG_GROUPS = 2
FFN_TILE = 1024
FFN_FCHUNK = 256
FFN_GROUPS = 2
ATT_TILE = 1024
ATT_GROUPS = 2
SAFE_LOG_DECAY = -80.0
VMEM_LIMIT = 56 * 1024 * 1024
SUBLANES = 8
LANES = 128


def _dot(a, b):
    return lax.dot_general(a, b, (((1,), (0,)), ((), ())), preferred_element_type=F32)


def _dot_nt(a, b):
    return lax.dot_general(a, b, (((1,), (1,)), ((), ())), preferred_element_type=F32)


def _rms(x, g):
    ms = jnp.mean(x * x, axis=-1, keepdims=True)
    return x * lax.rsqrt(ms + EPS) * g


def _sigmoid(x):
    return 1.0 / (1.0 + jnp.exp(-x))


def _resident(shape):
    nd = len(shape)
    return pl.BlockSpec(shape, lambda *_: (0,) * nd, pipeline_mode=pl.Buffered(1))


def _resident_layer(shape, layer):
    nd = len(shape)
    return pl.BlockSpec((None,) + tuple(shape[1:]), lambda *_: (layer,) + (0,) * (nd - 1),
                        pipeline_mode=pl.Buffered(1))


BF16_SUBLANES = 16


def _side_cast_specs(w, layer, n_seq_tiles, n_steps):
    _, rows, cols = w.shape
    rpb = next(r for r in range(BF16_SUBLANES, rows + 1, BF16_SUBLANES)
               if rows % r == 0 and rows // r <= n_steps)
    last = rows // rpb - 1

    def block(b, s):
        return jnp.minimum(b * n_seq_tiles + s, last)
    return (pl.BlockSpec((None, rpb, cols), lambda b, s: (layer, block(b, s), 0)),
            pl.BlockSpec((rpb, cols), lambda b, s: (block(b, s), 0)),
            jax.ShapeDtypeStruct((rows, cols), BF16))


def _hgrn_kernel(x_ref, g_ref, lbl_ref, win_ref, onorm_ref, wout_ref, cast_a_ref, cast_b_ref,
                 out_ref, cast_a_out, cast_b_out,
                 st_ref, stn_ref, q_s, k_s, v_s, cum_s, o_s, gate_s, qt_s, a_s, vt_s, inc_s, *, layer):
    T, D = q_s.shape
    oi_s = o_s
    cast_a_out[...] = cast_a_ref[...].astype(BF16)
    cast_b_out[...] = cast_b_ref[...].astype(BF16)
    C = HG_CHUNK
    n_chunks = T // C
    s_idx = pl.program_id(1)

    @pl.when(s_idx == 0)
    def _():
        st_ref[...] = jnp.zeros_like(st_ref)

    lg = lbl_ref[...]
    le = jnp.exp(lg - jnp.max(lg, axis=0, keepdims=True))
    lb = jnp.sum(le[0:layer + 1], axis=0, keepdims=True) / jnp.sum(le, axis=0, keepdims=True)

    row = lax.broadcasted_iota(jnp.int32, (C, C), 0)
    col = lax.broadcasted_iota(jnp.int32, (C, C), 1)
    causal = row >= col
    tri = jnp.where(causal, 1.0, 0.0).astype(BF16)
    tri2 = jnp.concatenate([tri, tri], axis=1)

    n_groups = HG_GROUPS if n_chunks % HG_GROUPS == 0 else 1
    cpg = n_chunks // n_groups

    def project(g):
        rows = slice(g * cpg * C, (g + 1) * cpg * C)
        xn = _rms(x_ref[0, rows, :], g_ref[layer:layer + 1, :]).astype(BF16)
        pq = _dot(xn, win_ref[:, 0:D])
        q_s[rows, :] = pq * _sigmoid(pq) * (HG_DK ** -0.5)
        pf = _dot(xn, win_ref[:, D:2 * D])
        forget = lb + (1.0 - lb) * _sigmoid(pf)
        k_s[rows, :] = 1.0 - forget
        logf = jnp.log(forget)
        v_s[rows, :] = _dot(xn, win_ref[:, 2 * D:3 * D])
        pg = _dot(xn, win_ref[:, 3 * D:4 * D])
        gate_s[rows, :] = pg * _sigmoid(pg)
        l_hi = logf.astype(BF16)
        l_lo = (logf - l_hi.astype(F32)).astype(BF16)
        for c in range(cpg):
            r = slice(c * C, (c + 1) * C)
            cum_s[(g * cpg + c) * C:(g * cpg + c + 1) * C, :] = _dot(
                tri2, jnp.concatenate([l_hi[r], l_lo[r]], axis=0))

    def stage(fast, chunks):
        for c in chunks:
            r = slice(c * C, (c + 1) * C)
            for h in range(HG_HEADS):
                hs = slice(h * HG_DK, (h + 1) * HG_DK)
                cum = cum_s[r, hs]
                last = cum[C - 1:C, :]
                qt = (q_s[r, hs] * jnp.exp(cum)).astype(BF16)
                vt = v_s[r, hs].T.astype(BF16)
                qt_s[r, hs] = qt
                vt_s[c * HG_HEADS + h] = vt
                if fast:
                    kt = k_s[r, hs] * jnp.exp(-cum)
                    a_s[r, hs] = jnp.where(causal, _dot_nt(qt, kt.astype(BF16)), 0.0).astype(BF16)
                    kd = (kt * jnp.exp(last)).astype(BF16)
                else:
                    kd = (k_s[r, hs] * jnp.exp(last - cum)).astype(BF16)
                inc_s[c * HG_HEADS + h] = _dot(vt, kd)

    def scan(fast, chunks, states):
        states = list(states)
        for c in chunks:
            r = slice(c * C, (c + 1) * C)
            for h in range(HG_HEADS):
                hs = slice(h * HG_DK, (h + 1) * HG_DK)
                st_b = states[h].astype(BF16)
                if fast:
                    o_s[r, hs] = _dot_nt(jnp.concatenate([a_s[r, hs], qt_s[r, hs]], axis=1),
                                         jnp.concatenate([vt_s[c * HG_HEADS + h], st_b], axis=1))
                else:
                    o_s[r, hs] = oi_s[r, hs] + _dot_nt(qt_s[r, hs], st_b)
                decay = jnp.exp(cum_s[(c + 1) * C - 1:(c + 1) * C, hs])
                states[h] = states[h] * decay + inc_s[c * HG_HEADS + h]
        return states

    def finish(g):
        rows = slice(g * cpg * C, (g + 1) * cpg * C)
        onorm = onorm_ref[layer:layer + 1, :]
        parts = []
        for h in range(HG_HEADS):
            hs = slice(h * HG_DK, (h + 1) * HG_DK)
            parts.append(_rms(o_s[rows, hs], onorm) * gate_s[rows, hs])
        o = jnp.concatenate(parts, axis=-1).astype(BF16)
        out_ref[0, rows, :] = x_ref[0, rows, :] + _dot(o, wout_ref[...])

    states = [st_ref[h] for h in range(HG_HEADS)]
    for g in range(n_groups):
        chunks = range(g * cpg, (g + 1) * cpg)
        project(g)
        stage(True, chunks)
        states = scan(True, chunks, states)
        finish(g)
    for h in range(HG_HEADS):
        stn_ref[h] = states[h]

    lasts = jnp.concatenate([cum_s[(c + 1) * C - 1:(c + 1) * C, :] for c in range(n_chunks)], axis=0)
    safe = jnp.min(lasts) > SAFE_LOG_DECAY

    @pl.when(jnp.logical_not(safe))
    def _():
        tcol = lax.broadcasted_iota(jnp.int32, (C, 1), 0)
        for c in range(n_chunks):
            r = slice(c * C, (c + 1) * C)
            qc = q_s[r, :]
            cumc = cum_s[r, :]

            def body(j, acc):
                kj = k_s[pl.ds(c * C + j, 1), :]
                vj = v_s[pl.ds(c * C + j, 1), :]
                cj = cum_s[pl.ds(c * C + j, 1), :]
                w = qc * kj * jnp.exp(jnp.minimum(cumc - cj, 0.0))
                keep = tcol >= j
                parts = []
                for h in range(HG_HEADS):
                    hs = slice(h * HG_DK, (h + 1) * HG_DK)
                    sc = jnp.sum(w[:, hs], axis=-1, keepdims=True)
                    parts.append(jnp.where(keep, sc, 0.0) * vj[:, hs])
                return acc + jnp.concatenate(parts, axis=-1)

            oi_s[r, :] = lax.fori_loop(0, C, body, jnp.zeros((C, D), F32))
        stage(False, range(n_chunks))
        redo = scan(False, range(n_chunks), [st_ref[h] for h in range(HG_HEADS)])
        for h in range(HG_HEADS):
            stn_ref[h] = redo[h]
        for g in range(n_groups):
            finish(g)

    st_ref[...] = stn_ref[...]


def _hgrn_layer(x, norm_g, lb_logits, w_in, out_norm, w_out, layer, cast_a, cast_b, cast_layer):
    B, S, D = x.shape
    T = min(HG_TILE, S)
    assert S % T == 0 and T % HG_CHUNK == 0 and D == HG_HEADS * HG_DK
    tile = pl.BlockSpec((1, T, D), lambda b, s: (b, s, 0))
    casts = [_side_cast_specs(w, cast_layer, S // T, B * (S // T)) for w in (cast_a, cast_b)]
    return pl.pallas_call(
        functools.partial(_hgrn_kernel, layer=layer),
        grid=(B, S // T),
        in_specs=[tile, _resident(norm_g.shape), _resident(lb_logits.shape), _resident_layer(w_in.shape, layer),
                  _resident(out_norm.shape), _resident_layer(w_out.shape, layer), casts[0][0], casts[1][0]],
        out_specs=(tile, casts[0][1], casts[1][1]),
        out_shape=(jax.ShapeDtypeStruct(x.shape, F32), casts[0][2], casts[1][2]),
        scratch_shapes=([pltpu.VMEM((HG_HEADS, HG_DK, HG_DK), F32)] * 2 + [pltpu.VMEM((T, D), F32)] * 6
                        + [pltpu.VMEM((T, D), BF16)] * 2
                        + [pltpu.VMEM((T // HG_CHUNK * HG_HEADS, HG_DK, HG_CHUNK), BF16),
                           pltpu.VMEM((T // HG_CHUNK * HG_HEADS, HG_DK, HG_DK), F32)]),
        compiler_params=pltpu.CompilerParams(dimension_semantics=("arbitrary", "arbitrary"),
                                             vmem_limit_bytes=VMEM_LIMIT),
        name="hgrn2_layer",
    )(x, norm_g, lb_logits, w_in, out_norm, w_out, cast_a, cast_b)


def _ffn_kernel(x_ref, g_ref, wup_ref, cw_ref, cb_ref, wdn_ref, fg_ref, out_ref, carry_ref, act_ref,
                *, layer, final_norm):
    T, F = act_ref.shape
    FC = FFN_FCHUNK
    s_idx = pl.program_id(1)
    rid = lax.broadcasted_iota(jnp.int32, (SUBLANES, 1), 0)
    n_groups = FFN_GROUPS if T % (FFN_GROUPS * SUBLANES) == 0 else 1
    TG = T // n_groups
    for g in range(n_groups):
        rows = slice(g * TG, (g + 1) * TG)
        x = x_ref[0, rows, :]
        xn = _rms(x, g_ref[layer:layer + 1, :]).astype(BF16)
        for j in range(F // FC):
            cs = slice(j * FC, (j + 1) * FC)
            gate = _dot(xn, wup_ref[:, cs])
            val = _dot(xn, wup_ref[:, F + j * FC:F + (j + 1) * FC])
            prev = carry_ref[:, cs]
            if g == 0:
                prev = jnp.where(s_idx > 0, prev, 0.0)
            carry_ref[:, cs] = gate[TG - SUBLANES:, :]
            p1 = prev[SUBLANES - 1:SUBLANES, :]
            p2 = prev[SUBLANES - 2:SUBLANES - 1, :]
            g1 = pltpu.roll(gate, 1, 0)
            g2 = pltpu.roll(gate, 2, 0)
            g1 = jnp.concatenate([jnp.where(rid == 0, p1, g1[:SUBLANES]), g1[SUBLANES:]], axis=0)
            g2 = jnp.concatenate(
                [jnp.where(rid == 0, p2, jnp.where(rid == 1, p1, g2[:SUBLANES])), g2[SUBLANES:]], axis=0)
            conv = (cb_ref[layer:layer + 1, cs] + cw_ref[0:1, cs] * g2 + cw_ref[1:2, cs] * g1
                    + cw_ref[2:3, cs] * gate)
            act_ref[rows, cs] = (conv * _sigmoid(conv) * val).astype(BF16)
        h = x + _dot(act_ref[rows, :], wdn_ref[...])
        if final_norm:
            h = _rms(h, fg_ref[...])
        out_ref[0, rows, :] = h


def _ffn_layer(x, norm_g, w_up, conv_w, conv_b, w_down, final_g, layer, final_norm):
    B, S, D = x.shape
    F = w_down.shape[0]
    T = min(FFN_TILE, S)
    assert S % T == 0 and F % FFN_FCHUNK == 0 and conv_w.shape[1] == CONV_WIDTH
    tile = pl.BlockSpec((1, T, D), lambda b, s: (b, s, 0))
    return pl.pallas_call(
        functools.partial(_ffn_kernel, layer=layer, final_norm=final_norm),
        grid=(B, S // T),
        in_specs=[tile, _resident(norm_g.shape), _resident(w_up.shape),
                  _resident_layer(conv_w.shape, layer), _resident(conv_b.shape),
                  _resident(w_down.shape), _resident((1, D))],
        out_specs=tile,
        out_shape=jax.ShapeDtypeStruct(x.shape, F32),
        scratch_shapes=[pltpu.VMEM((SUBLANES, F), F32), pltpu.VMEM((T, F), BF16)],
        compiler_params=pltpu.CompilerParams(dimension_semantics=("arbitrary", "arbitrary"),
                                             vmem_limit_bytes=VMEM_LIMIT),
        name="conv_ffn_final" if final_norm else "conv_ffn",
    )(x, norm_g, w_up, conv_w, conv_b, w_down, final_g.reshape(1, D))


def _attn_kernel(sink_ref, h_ref, ag_ref, kg_ref, wq_ref, wkv_ref, wo_ref, cast_a_ref, cast_b_ref,
                 out_ref, cast_a_out, cast_b_out,
                 kprev_ref, vprev_ref, bias_s, sinkb_s, attn_s, *, layer):
    T, D = attn_s.shape
    cast_a_out[...] = cast_a_ref[...].astype(BF16)
    cast_b_out[...] = cast_b_ref[...].astype(BF16)
    W = WINDOW
    HD = ATT_HEAD_DIM
    KVD = ATT_KV_HEADS * HD
    pairs = ATT_GROUP // 2
    s_idx = pl.program_id(1)

    @pl.when(s_idx == 0)
    def _():
        kprev_ref[...] = jnp.zeros_like(kprev_ref)
        vprev_ref[...] = jnp.zeros_like(vprev_ref)

    R = pairs * W
    ii = lax.broadcasted_iota(jnp.int32, (R, W), 0) & (W - 1)
    jj = lax.broadcasted_iota(jnp.int32, (R, W), 1)
    upper = jj > ii
    dist = jnp.where(upper, ii - jj + W, ii - jj).astype(F32)
    piece = lax.broadcasted_iota(jnp.int32, (R, W), 0) // W
    for kvh in range(ATT_KV_HEADS):
        for parity in range(2):
            slope = jnp.zeros((R, W), F32)
            sink = jnp.zeros((R, W), F32)
            for m in range(pairs):
                hq = kvh * ATT_GROUP + 2 * m + parity
                slope = jnp.where(piece == m, 2.0 ** (-8.0 * (hq + 1) / ATT_Q_HEADS), slope)
                sink = jnp.where(piece == m, sink_ref[layer, hq], sink)
            bias_s[2 * kvh + parity] = -slope * dist
            sinkb_s[2 * kvh + parity] = sink
    first_mask = jnp.where(upper & (s_idx == 0), -jnp.inf, 0.0)

    lane = lax.broadcasted_iota(jnp.int32, (1, KVD), 1)
    low = lane < HD
    high = jnp.logical_not(low)

    def padded(t_all, t_rot, kvh, parity):
        src = t_all if kvh == parity else t_rot
        return jnp.where(low if parity == 0 else high, src, 0.0).astype(BF16)

    ones = [jnp.broadcast_to(jnp.where(sel, 1.0, 0.0), (2 * W, KVD)).astype(BF16) for sel in (low, high)]

    n_groups = ATT_GROUPS if T % (ATT_GROUPS * W) == 0 else 1
    TG = T // n_groups
    for g in range(n_groups):
        grows = slice(g * TG, (g + 1) * TG)
        h = h_ref[0, grows, :]
        hn = h * lax.rsqrt(jnp.mean(h * h, axis=-1, keepdims=True) + EPS)
        q = (_dot((hn * ag_ref[layer:layer + 1, :]).astype(BF16), wq_ref[...]) * (HD ** -0.5)).astype(BF16)
        kv = _dot((hn * kg_ref[...]).astype(BF16), wkv_ref[...])
        k_all = jnp.concatenate([kprev_ref[...], kv[:, :KVD]], axis=0)
        v_all = jnp.concatenate([vprev_ref[...], kv[:, KVD:]], axis=0)
        kprev_ref[...] = kv[TG - W:, :KVD]
        vprev_ref[...] = kv[TG - W:, KVD:]
        k_rot = pltpu.roll(k_all, HD, 1)
        v_rot = pltpu.roll(v_all, HD, 1)

        for kvh in range(ATT_KV_HEADS):
            k_pad = [padded(k_all, k_rot, kvh, parity) for parity in range(2)]
            v_pad = [padded(v_all, v_rot, kvh, parity) for parity in range(2)]
            for n in range(TG // W):
                rows = slice(n * W, (n + 1) * W)
                band = slice(n * W, (n + 2) * W)
                qs = jnp.concatenate(
                    [q[rows, (kvh * pairs + m) * LANES:(kvh * pairs + m + 1) * LANES] for m in range(pairs)],
                    axis=0)
                probs, sink_terms = [], []
                for parity in range(2):
                    sc = _dot_nt(qs, k_pad[parity][band])
                    f = jnp.where(upper, sc[:, :W], sc[:, W:]) + bias_s[2 * kvh + parity]
                    if g == 0 and n == 0:
                        f = f + first_mask
                    mx = jnp.broadcast_to(jnp.max(f, axis=-1, keepdims=True), (R, W))
                    e = jnp.exp(f - mx)
                    sink_terms.append(jnp.exp(sinkb_s[2 * kvh + parity] - mx))
                    probs.append(jnp.concatenate([jnp.where(upper, e, 0.0), jnp.where(upper, 0.0, e)],
                                                 axis=1).astype(BF16))
                lhs = jnp.concatenate(probs, axis=1)
                rhs = jnp.concatenate(
                    [jnp.concatenate([v_pad[parity][band], ones[parity]], axis=1) for parity in range(2)],
                    axis=0)
                pv = _dot(lhs, rhs)
                out = pv[:, :KVD] / (pv[:, KVD:] + jnp.where(low, sink_terms[0], sink_terms[1]))
                for m in range(pairs):
                    cols = slice((kvh * pairs + m) * LANES, (kvh * pairs + m + 1) * LANES)
                    attn_s[g * TG + n * W:g * TG + (n + 1) * W, cols] = out[m * W:(m + 1) * W]

        out_ref[0, grows, :] = h + _dot(attn_s[grows, :].astype(BF16), wo_ref[...])


def _attn_layer(h, attn_g, kv_g, w_q, w_kv, sinks, w_o, layer, cast_a, cast_b, cast_layer):
    B, S, D = h.shape
    T = min(ATT_TILE, S)
    assert S % T == 0 and T % WINDOW == 0
    assert w_q.shape[2] == ATT_Q_HEADS * ATT_HEAD_DIM and w_kv.shape[1] == 2 * ATT_KV_HEADS * ATT_HEAD_DIM
    tile = pl.BlockSpec((1, T, D), lambda b, s: (b, s, 0))
    kvd = ATT_KV_HEADS * ATT_HEAD_DIM
    casts = [_side_cast_specs(w, cast_layer, S // T, B * (S // T)) for w in (cast_a, cast_b)]
    return pl.pallas_call(
        functools.partial(_attn_kernel, layer=layer),
        grid=(B, S // T),
        in_specs=[pl.BlockSpec(memory_space=pltpu.SMEM), tile, _resident(attn_g.shape), _resident((1, D)),
                  _resident_layer(w_q.shape, layer), _resident(w_kv.shape), _resident_layer(w_o.shape, layer),
                  casts[0][0], casts[1][0]],
        out_specs=(tile, casts[0][1], casts[1][1]),
        out_shape=(jax.ShapeDtypeStruct(h.shape, F32), casts[0][2], casts[1][2]),
        scratch_shapes=[pltpu.VMEM((WINDOW, kvd), F32), pltpu.VMEM((WINDOW, kvd), F32),
                        pltpu.VMEM((2 * ATT_KV_HEADS, ATT_GROUP // 2 * WINDOW, WINDOW), F32),
                        pltpu.VMEM((2 * ATT_KV_HEADS, ATT_GROUP // 2 * WINDOW, WINDOW), F32),
                        pltpu.VMEM((T, ATT_Q_HEADS * ATT_HEAD_DIM), F32)],
        compiler_params=pltpu.CompilerParams(dimension_semantics=("arbitrary", "arbitrary"),
                                             vmem_limit_bytes=VMEM_LIMIT),
        name="swa_layer",
    )(sinks, h, attn_g, kv_g.reshape(1, D), w_q, w_kv, w_o, cast_a, cast_b)


def kernel(x, hg_norm, hg_w_in, hg_lb_logits, hg_out_norm, hg_w_out, kv_norm, w_kv, attn_norm, attn_w_q,
           attn_sinks, attn_w_o, ffn_norm, ffn_w_up, ffn_conv_w, ffn_conv_b, ffn_w_down, final_norm):
    depth = ffn_norm.shape[0]
    n_a = hg_norm.shape[0]
    assert depth - n_a == 1
    h = x
    for layer in range(depth):
        if layer < n_a:
            h, w_up, w_down = _hgrn_layer(h, hg_norm, hg_lb_logits, hg_w_in, hg_out_norm, hg_w_out, layer,
                                          ffn_w_up, ffn_w_down, layer)
        else:
            h, w_up, w_down = _attn_layer(h, attn_norm, kv_norm, attn_w_q, w_kv, attn_sinks, attn_w_o,
                                          layer - n_a, ffn_w_up, ffn_w_down, layer)
        h = _ffn_layer(h, ffn_norm, w_up, ffn_conv_w, ffn_conv_b, w_down, final_norm, layer,
                       layer == depth - 1)
    return h
```

```python
import functools

import jax
import jax.numpy as jnp
from jax import lax
from jax.experimental import pallas as pl
from jax.experimental.pallas import tpu as pltpu

F32 = jnp.float32
BF16 = jnp.bfloat16
EPS = 1e-6

HG_HEADS = 8
HG_DK = 128
ATT_HEAD_DIM = 64
ATT_Q_HEADS = 16
ATT_KV_HEADS = 2
ATT_GROUP = ATT_Q_HEADS // ATT_KV_HEADS
WINDOW = 128
CONV_WIDTH = 3

HG_TILE = 512
HG_CHUNK = 128
FFN_TILE = 1024
ATT_TILE = 1024
SAFE_LOG_DECAY = -80.0

V7X_VMEM_BYTES = 64 * 1024 * 1024
V7X_MXU_WIDTH = 256
SUBLANES = 8
BF16_SUBLANES = 16
LANES = 128

FFN_FCHUNK = V7X_MXU_WIDTH
VMEM_LIMIT = V7X_VMEM_BYTES * 7 // 8


def _dot(a, b):
    return lax.dot_general(a, b, (((1,), (0,)), ((), ())), preferred_element_type=F32)


def _dot_nt(a, b):
    return lax.dot_general(a, b, (((1,), (1,)), ((), ())), preferred_element_type=F32)


def _rms(x, g):
    ms = jnp.mean(x * x, axis=-1, keepdims=True)
    return x * lax.rsqrt(ms + EPS) * g


def _sigmoid(x):
    return 1.0 / (1.0 + jnp.exp(-x))


def _silu_tanh(x):
    h = 0.5 * x
    return h + h * jnp.tanh(h)


def _resident(shape):
    nd = len(shape)
    return pl.BlockSpec(shape, lambda *_: (0,) * nd, pipeline_mode=pl.Buffered(1))


def _resident_layer(shape, layer):
    nd = len(shape)
    return pl.BlockSpec((None,) + tuple(shape[1:]), lambda *_: (layer,) + (0,) * (nd - 1),
                        pipeline_mode=pl.Buffered(1))


def _side_cast_specs(w, layer, n_seq_tiles, n_steps):
    _, rows, cols = w.shape
    rpb = next(r for r in range(BF16_SUBLANES, rows + 1, BF16_SUBLANES)
               if rows % r == 0 and rows // r <= n_steps)
    last = rows // rpb - 1

    def block(b, s):
        return jnp.minimum(b * n_seq_tiles + s, last)
    return (pl.BlockSpec((None, rpb, cols), lambda b, s: (layer, block(b, s), 0)),
            pl.BlockSpec((rpb, cols), lambda b, s: (block(b, s), 0)),
            jax.ShapeDtypeStruct((rows, cols), BF16))


def _hgrn_kernel(x_ref, g_ref, lbl_ref, win_ref, onorm_ref, wout_ref, cast_a_ref, cast_b_ref,
                 out_ref, cast_a_out, cast_b_out,
                 st_ref, stn_ref, q_s, k_s, v_s, cum_s, o_s, qt_s, a_s, vt_s, inc_s, *, layer):
    T, D = q_s.shape
    oi_s = o_s
    cast_a_out[...] = cast_a_ref[...].astype(BF16)
    cast_b_out[...] = cast_b_ref[...].astype(BF16)
    C = HG_CHUNK
    n_chunks = T // C
    s_idx = pl.program_id(1)

    @pl.when(s_idx == 0)
    def _():
        st_ref[...] = jnp.zeros_like(st_ref)

    x = x_ref[0]
    xn = _rms(x, g_ref[layer:layer + 1, :]).astype(BF16)

    lg = lbl_ref[...]
    le = jnp.exp(lg - jnp.max(lg, axis=0, keepdims=True))
    lb = jnp.sum(le[0:layer + 1], axis=0, keepdims=True) / jnp.sum(le, axis=0, keepdims=True)

    pq = _dot(xn, win_ref[:, 0:D])
    q_s[...] = _silu_tanh(pq) * (HG_DK ** -0.5)
    pf = _dot(xn, win_ref[:, D:2 * D])
    forget = lb + (1.0 - lb) * (0.5 + 0.5 * jnp.tanh(0.5 * pf))
    k_s[...] = 1.0 - forget
    logf = jnp.log(forget)
    v_s[...] = _dot(xn, win_ref[:, 2 * D:3 * D])
    pg = _dot(xn, win_ref[:, 3 * D:4 * D])
    gate = _silu_tanh(pg)

    row = lax.broadcasted_iota(jnp.int32, (C, C), 0)
    col = lax.broadcasted_iota(jnp.int32, (C, C), 1)
    causal = row >= col
    tri = jnp.where(causal, 1.0, 0.0).astype(BF16)
    l_hi = logf.astype(BF16)
    l_lo = (logf - l_hi.astype(F32)).astype(BF16)
    tri2 = jnp.concatenate([tri, tri], axis=1)
    for c in range(n_chunks):
        r = slice(c * C, (c + 1) * C)
        cum_s[r, :] = _dot(tri2, jnp.concatenate([l_hi[r], l_lo[r]], axis=0))

    lasts = jnp.concatenate([cum_s[(c + 1) * C - 1:(c + 1) * C, :] for c in range(n_chunks)], axis=0)
    safe = jnp.min(lasts) > SAFE_LOG_DECAY

    def stage(fast):
        for c in range(n_chunks):
            r = slice(c * C, (c + 1) * C)
            for h in range(HG_HEADS):
                hs = slice(h * HG_DK, (h + 1) * HG_DK)
                cum = cum_s[r, hs]
                last = cum[C - 1:C, :]
                qt = (q_s[r, hs] * jnp.exp(cum)).astype(BF16)
                vt = v_s[r, hs].T.astype(BF16)
                qt_s[r, hs] = qt
                vt_s[c * HG_HEADS + h] = vt
                if fast:
                    kt = k_s[r, hs] * jnp.exp(-cum)
                    a_s[r, hs] = jnp.where(causal, _dot_nt(qt, kt.astype(BF16)), 0.0).astype(BF16)
                    kd = (kt * jnp.exp(last)).astype(BF16)
                else:
                    kd = (k_s[r, hs] * jnp.exp(last - cum)).astype(BF16)
                inc_s[c * HG_HEADS + h] = _dot(vt, kd)

    def scan(fast):
        states = [st_ref[h] for h in range(HG_HEADS)]
        for c in range(n_chunks):
            r = slice(c * C, (c + 1) * C)
            for h in range(HG_HEADS):
                hs = slice(h * HG_DK, (h + 1) * HG_DK)
                st_b = states[h].astype(BF16)
                if fast:
                    o_s[r, hs] = _dot_nt(jnp.concatenate([a_s[r, hs], qt_s[r, hs]], axis=1),
                                         jnp.concatenate([vt_s[c * HG_HEADS + h], st_b], axis=1))
                else:
                    o_s[r, hs] = oi_s[r, hs] + _dot_nt(qt_s[r, hs], st_b)
                decay = jnp.exp(cum_s[(c + 1) * C - 1:(c + 1) * C, hs])
                states[h] = states[h] * decay + inc_s[c * HG_HEADS + h]
        for h in range(HG_HEADS):
            stn_ref[h] = states[h]

    def finish():
        onorm = onorm_ref[layer:layer + 1, :]
        parts = []
        for h in range(HG_HEADS):
            hs = slice(h * HG_DK, (h + 1) * HG_DK)
            parts.append(_rms(o_s[:, hs], onorm) * gate[:, hs])
        o = jnp.concatenate(parts, axis=-1).astype(BF16)
        out_ref[0] = x + _dot(o, wout_ref[...])

    stage(True)
    scan(True)
    finish()

    @pl.when(jnp.logical_not(safe))
    def _():
        tcol = lax.broadcasted_iota(jnp.int32, (C, 1), 0)
        for c in range(n_chunks):
            r = slice(c * C, (c + 1) * C)
            qc = q_s[r, :]
            cumc = cum_s[r, :]

            def body(j, acc):
                kj = k_s[pl.ds(c * C + j, 1), :]
                vj = v_s[pl.ds(c * C + j, 1), :]
                cj = cum_s[pl.ds(c * C + j, 1), :]
                w = qc * kj * jnp.exp(jnp.minimum(cumc - cj, 0.0))
                keep = tcol >= j
                parts = []
                for h in range(HG_HEADS):
                    hs = slice(h * HG_DK, (h + 1) * HG_DK)
                    sc = jnp.sum(w[:, hs], axis=-1, keepdims=True)
                    parts.append(jnp.where(keep, sc, 0.0) * vj[:, hs])
                return acc + jnp.concatenate(parts, axis=-1)

            oi_s[r, :] = lax.fori_loop(0, C, body, jnp.zeros((C, D), F32))
        stage(False)
        scan(False)
        finish()

    st_ref[...] = stn_ref[...]


def _hgrn_layer(x, norm_g, lb_logits, w_in, out_norm, w_out, layer, cast_a, cast_b, cast_layer):
    B, S, D = x.shape
    T = min(HG_TILE, S)
    assert S % T == 0 and T % HG_CHUNK == 0 and D == HG_HEADS * HG_DK
    tile = pl.BlockSpec((1, T, D), lambda b, s: (b, s, 0))
    casts = [_side_cast_specs(w, cast_layer, S // T, B * (S // T)) for w in (cast_a, cast_b)]
    return pl.pallas_call(
        functools.partial(_hgrn_kernel, layer=layer),
        grid=(B, S // T),
        in_specs=[tile, _resident(norm_g.shape), _resident(lb_logits.shape), _resident_layer(w_in.shape, layer),
                  _resident(out_norm.shape), _resident_layer(w_out.shape, layer), casts[0][0], casts[1][0]],
        out_specs=(tile, casts[0][1], casts[1][1]),
        out_shape=(jax.ShapeDtypeStruct(x.shape, F32), casts[0][2], casts[1][2]),
        scratch_shapes=([pltpu.VMEM((HG_HEADS, HG_DK, HG_DK), F32)] * 2 + [pltpu.VMEM((T, D), F32)] * 5
                        + [pltpu.VMEM((T, D), BF16)] * 2
                        + [pltpu.VMEM((T // HG_CHUNK * HG_HEADS, HG_DK, HG_CHUNK), BF16),
                           pltpu.VMEM((T // HG_CHUNK * HG_HEADS, HG_DK, HG_DK), F32)]),
        compiler_params=pltpu.CompilerParams(dimension_semantics=("arbitrary", "arbitrary"),
                                             vmem_limit_bytes=VMEM_LIMIT),
        name="hgrn2_layer",
    )(x, norm_g, lb_logits, w_in, out_norm, w_out, cast_a, cast_b)


def _ffn_kernel(x_ref, g_ref, wup_ref, cw_ref, cb_ref, wdn_ref, fg_ref, out_ref, carry_ref, act_ref,
                *, layer, final_norm):
    T, F = act_ref.shape
    FC = FFN_FCHUNK
    s_idx = pl.program_id(1)
    x = x_ref[0]
    xn = _rms(x, g_ref[layer:layer + 1, :]).astype(BF16)
    has_prev = s_idx > 0
    rid = lax.broadcasted_iota(jnp.int32, (SUBLANES, 1), 0)
    for j in range(F // FC):
        cs = slice(j * FC, (j + 1) * FC)
        gate = _dot(xn, wup_ref[:, cs])
        val = _dot(xn, wup_ref[:, F + j * FC:F + (j + 1) * FC])
        prev = jnp.where(has_prev, carry_ref[:, cs], 0.0)
        carry_ref[:, cs] = gate[T - SUBLANES:, :]
        p1 = prev[SUBLANES - 1:SUBLANES, :]
        p2 = prev[SUBLANES - 2:SUBLANES - 1, :]
        g1 = pltpu.roll(gate, 1, 0)
        g2 = pltpu.roll(gate, 2, 0)
        g1 = jnp.concatenate([jnp.where(rid == 0, p1, g1[:SUBLANES]), g1[SUBLANES:]], axis=0)
        g2 = jnp.concatenate([jnp.where(rid == 0, p2, jnp.where(rid == 1, p1, g2[:SUBLANES])), g2[SUBLANES:]],
                             axis=0)
        conv = cb_ref[layer:layer + 1, cs] + cw_ref[0:1, cs] * g2 + cw_ref[1:2, cs] * g1 + cw_ref[2:3, cs] * gate
        act_ref[:, cs] = (conv * _sigmoid(conv) * val).astype(BF16)
    h = x + _dot(act_ref[...], wdn_ref[...])
    if final_norm:
        h = _rms(h, fg_ref[...])
    out_ref[0] = h


def _ffn_layer(x, norm_g, w_up, conv_w, conv_b, w_down, final_g, layer, final_norm):
    B, S, D = x.shape
    F = w_down.shape[0]
    T = min(FFN_TILE, S)
    assert S % T == 0 and F % FFN_FCHUNK == 0 and conv_w.shape[1] == CONV_WIDTH
    tile = pl.BlockSpec((1, T, D), lambda b, s: (b, s, 0))
    return pl.pallas_call(
        functools.partial(_ffn_kernel, layer=layer, final_norm=final_norm),
        grid=(B, S // T),
        in_specs=[tile, _resident(norm_g.shape), _resident(w_up.shape),
                  _resident_layer(conv_w.shape, layer), _resident(conv_b.shape),
                  _resident(w_down.shape), _resident((1, D))],
        out_specs=tile,
        out_shape=jax.ShapeDtypeStruct(x.shape, F32),
        scratch_shapes=[pltpu.VMEM((SUBLANES, F), F32), pltpu.VMEM((T, F), BF16)],
        compiler_params=pltpu.CompilerParams(dimension_semantics=("arbitrary", "arbitrary"),
                                             vmem_limit_bytes=VMEM_LIMIT),
        name="conv_ffn_final" if final_norm else "conv_ffn",
    )(x, norm_g, w_up, conv_w, conv_b, w_down, final_g.reshape(1, D))


def _attn_kernel(sink_ref, h_ref, ag_ref, kg_ref, wq_ref, wkv_ref, wo_ref, cast_a_ref, cast_b_ref,
                 out_ref, cast_a_out, cast_b_out,
                 kprev_ref, vprev_ref, bias_s, sinkb_s, attn_s, *, layer):
    T, D = attn_s.shape
    cast_a_out[...] = cast_a_ref[...].astype(BF16)
    cast_b_out[...] = cast_b_ref[...].astype(BF16)
    W = WINDOW
    HD = ATT_HEAD_DIM
    KVD = ATT_KV_HEADS * HD
    pairs = ATT_GROUP // 2
    s_idx = pl.program_id(1)

    @pl.when(s_idx == 0)
    def _():
        kprev_ref[...] = jnp.zeros_like(kprev_ref)
        vprev_ref[...] = jnp.zeros_like(vprev_ref)

    h = h_ref[0]
    hn = h * lax.rsqrt(jnp.mean(h * h, axis=-1, keepdims=True) + EPS)
    q = (_dot((hn * ag_ref[layer:layer + 1, :]).astype(BF16), wq_ref[...]) * (HD ** -0.5)).astype(BF16)
    kv = _dot((hn * kg_ref[...]).astype(BF16), wkv_ref[...])
    k_all = jnp.concatenate([kprev_ref[...], kv[:, :KVD]], axis=0)
    v_all = jnp.concatenate([vprev_ref[...], kv[:, KVD:]], axis=0)
    kprev_ref[...] = kv[T - W:, :KVD]
    vprev_ref[...] = kv[T - W:, KVD:]

    R = pairs * W
    ii = lax.broadcasted_iota(jnp.int32, (R, W), 0) & (W - 1)
    jj = lax.broadcasted_iota(jnp.int32, (R, W), 1)
    upper = jj > ii
    dist = jnp.where(upper, ii - jj + W, ii - jj).astype(F32)
    piece = lax.broadcasted_iota(jnp.int32, (R, W), 0) // W
    for kvh in range(ATT_KV_HEADS):
        for parity in range(2):
            slope = jnp.zeros((R, W), F32)
            sink = jnp.zeros((R, W), F32)
            for m in range(pairs):
                hq = kvh * ATT_GROUP + 2 * m + parity
                slope = jnp.where(piece == m, 2.0 ** (-8.0 * (hq + 1) / ATT_Q_HEADS), slope)
                sink = jnp.where(piece == m, sink_ref[layer, hq], sink)
            bias_s[2 * kvh + parity] = -slope * dist
            sinkb_s[2 * kvh + parity] = sink
    first_mask = jnp.where(upper & (s_idx == 0), -jnp.inf, 0.0)

    lane = lax.broadcasted_iota(jnp.int32, (1, KVD), 1)
    low = lane < HD
    high = jnp.logical_not(low)
    k_rot = pltpu.roll(k_all, HD, 1)
    v_rot = pltpu.roll(v_all, HD, 1)

    def padded(t_all, t_rot, kvh, parity):
        src = t_all if kvh == parity else t_rot
        return jnp.where(low if parity == 0 else high, src, 0.0).astype(BF16)

    ones = [jnp.broadcast_to(jnp.where(sel, 1.0, 0.0), (2 * W, KVD)).astype(BF16) for sel in (low, high)]

    for kvh in range(ATT_KV_HEADS):
        k_pad = [padded(k_all, k_rot, kvh, parity) for parity in range(2)]
        v_pad = [padded(v_all, v_rot, kvh, parity) for parity in range(2)]
        for n in range(T // W):
            rows = slice(n * W, (n + 1) * W)
            band = slice(n * W, (n + 2) * W)
            qs = jnp.concatenate(
                [q[rows, (kvh * pairs + m) * LANES:(kvh * pairs + m + 1) * LANES] for m in range(pairs)],
                axis=0)
            probs, sink_terms = [], []
            for parity in range(2):
                sc = _dot_nt(qs, k_pad[parity][band])
                f = jnp.where(upper, sc[:, :W], sc[:, W:]) + bias_s[2 * kvh + parity]
                if n == 0:
                    f = f + first_mask
                mx = jnp.broadcast_to(jnp.max(f, axis=-1, keepdims=True), (R, W))
                e = jnp.exp(f - mx)
                sink_terms.append(jnp.exp(sinkb_s[2 * kvh + parity] - mx))
                probs.append(jnp.concatenate([jnp.where(upper, e, 0.0), jnp.where(upper, 0.0, e)],
                                             axis=1).astype(BF16))
            lhs = jnp.concatenate(probs, axis=1)
            rhs = jnp.concatenate(
                [jnp.concatenate([v_pad[parity][band], ones[parity]], axis=1) for parity in range(2)],
                axis=0)
            pv = _dot(lhs, rhs)
            out = pv[:, :KVD] / (pv[:, KVD:] + jnp.where(low, sink_terms[0], sink_terms[1]))
            for m in range(pairs):
                cols = slice((kvh * pairs + m) * LANES, (kvh * pairs + m + 1) * LANES)
                attn_s[rows, cols] = out[m * W:(m + 1) * W]

    out_ref[0] = h + _dot(attn_s[...].astype(BF16), wo_ref[...])


def _attn_layer(h, attn_g, kv_g, w_q, w_kv, sinks, w_o, layer, cast_a, cast_b, cast_layer):
    B, S, D = h.shape
    T = min(ATT_TILE, S)
    assert S % T == 0 and T % WINDOW == 0
    assert w_q.shape[2] == ATT_Q_HEADS * ATT_HEAD_DIM and w_kv.shape[1] == 2 * ATT_KV_HEADS * ATT_HEAD_DIM
    tile = pl.BlockSpec((1, T, D), lambda b, s: (b, s, 0))
    kvd = ATT_KV_HEADS * ATT_HEAD_DIM
    casts = [_side_cast_specs(w, cast_layer, S // T, B * (S // T)) for w in (cast_a, cast_b)]
    return pl.pallas_call(
        functools.partial(_attn_kernel, layer=layer),
        grid=(B, S // T),
        in_specs=[pl.BlockSpec(memory_space=pltpu.SMEM), tile, _resident(attn_g.shape), _resident((1, D)),
                  _resident_layer(w_q.shape, layer), _resident(w_kv.shape), _resident_layer(w_o.shape, layer),
                  casts[0][0], casts[1][0]],
        out_specs=(tile, casts[0][1], casts[1][1]),
        out_shape=(jax.ShapeDtypeStruct(h.shape, F32), casts[0][2], casts[1][2]),
        scratch_shapes=[pltpu.VMEM((WINDOW, kvd), F32), pltpu.VMEM((WINDOW, kvd), F32),
                        pltpu.VMEM((2 * ATT_KV_HEADS, ATT_GROUP // 2 * WINDOW, WINDOW), F32),
                        pltpu.VMEM((2 * ATT_KV_HEADS, ATT_GROUP // 2 * WINDOW, WINDOW), F32),
                        pltpu.VMEM((T, ATT_Q_HEADS * ATT_HEAD_DIM), F32)],
        compiler_params=pltpu.CompilerParams(dimension_semantics=("arbitrary", "arbitrary"),
                                             vmem_limit_bytes=VMEM_LIMIT),
        name="swa_layer",
    )(sinks, h, attn_g, kv_g.reshape(1, D), w_q, w_kv, w_o, cast_a, cast_b)


def kernel(x, hg_norm, hg_w_in, hg_lb_logits, hg_out_norm, hg_w_out, kv_norm, w_kv, attn_norm, attn_w_q,
           attn_sinks, attn_w_o, ffn_norm, ffn_w_up, ffn_conv_w, ffn_conv_b, ffn_w_down, final_norm):
    depth = ffn_norm.shape[0]
    n_a = hg_norm.shape[0]
    assert depth - n_a == 1
    h = x
    for layer in range(depth):
        if layer < n_a:
            h, w_up, w_down = _hgrn_layer(h, hg_norm, hg_lb_logits, hg_w_in, hg_out_norm, hg_w_out, layer,
                                          ffn_w_up, ffn_w_down, layer)
        else:
            h, w_up, w_down = _attn_layer(h, attn_norm, kv_norm, attn_w_q, w_kv, attn_sinks, attn_w_o,
                                          layer - n_a, ffn_w_up, ffn_w_down, layer)
        h = _ffn_layer(h, ffn_norm, w_up, ffn_conv_w, ffn_conv_b, w_down, final_norm, layer,
                       layer == depth - 1)
    return h
```

```python
import functools

import jax
import jax.numpy as jnp
from jax import lax
from jax.experimental import pallas as pl
from jax.experimental.pallas import tpu as pltpu

F32 = jnp.float32
BF16 = jnp.bfloat16
EPS = 1e-6

HG_HEADS = 8
HG_DK = 128
ATT_HEAD_DIM = 64
ATT_Q_HEADS = 16
ATT_KV_HEADS = 2
ATT_GROUP = ATT_Q_HEADS // ATT_KV_HEADS
WINDOW = 128
CONV_WIDTH = 3

HG_TILE = 512
HG_CHUNK = 128
FFN_TILE = 1024
ATT_TILE = 1024
SAFE_LOG_DECAY = -80.0

V7X_VMEM_BYTES = 64 * 1024 * 1024
V7X_MXU_WIDTH = 256
SUBLANES = 8
BF16_SUBLANES = 16
LANES = 128

FFN_FCHUNK = V7X_MXU_WIDTH
VMEM_LIMIT = V7X_VMEM_BYTES * 7 // 8


def _dot(a, b):
    return lax.dot_general(a, b, (((1,), (0,)), ((), ())), preferred_element_type=F32)


def _dot_nt(a, b):
    return lax.dot_general(a, b, (((1,), (1,)), ((), ())), preferred_element_type=F32)


def _rms(x, g):
    ms = jnp.mean(x * x, axis=-1, keepdims=True)
    return x * lax.rsqrt(ms + EPS) * g


def _sigmoid(x):
    return 1.0 / (1.0 + jnp.exp(-x))


def _silu_tanh(x):
    h = 0.5 * x
    return h + h * jnp.tanh(h)


def _resident(shape):
    nd = len(shape)
    return pl.BlockSpec(shape, lambda *_: (0,) * nd, pipeline_mode=pl.Buffered(1))


def _resident_layer(shape, layer):
    nd = len(shape)
    return pl.BlockSpec((None,) + tuple(shape[1:]), lambda *_: (layer,) + (0,) * (nd - 1),
                        pipeline_mode=pl.Buffered(1))


def _side_cast_specs(w, layer, n_seq_tiles, n_steps):
    _, rows, cols = w.shape
    rpb = next(r for r in range(BF16_SUBLANES, rows + 1, BF16_SUBLANES)
               if rows % r == 0 and rows // r <= n_steps)
    last = rows // rpb - 1

    def block(b, s):
        return jnp.minimum(b * n_seq_tiles + s, last)
    return (pl.BlockSpec((None, rpb, cols), lambda b, s: (layer, block(b, s), 0)),
            pl.BlockSpec((rpb, cols), lambda b, s: (block(b, s), 0)),
            jax.ShapeDtypeStruct((rows, cols), BF16))


def _hgrn_kernel(x_ref, g_ref, lbl_ref, win_ref, onorm_ref, wout_ref, cast_a_ref, cast_b_ref,
                 out_ref, cast_a_out, cast_b_out,
                 st_ref, stn_ref, q_s, k_s, v_s, cum_s, o_s, qt_s, a_s, vt_s, inc_s, *, layer):
    T, D = q_s.shape
    oi_s = o_s
    cast_a_out[...] = cast_a_ref[...].astype(BF16)
    cast_b_out[...] = cast_b_ref[...].astype(BF16)
    C = HG_CHUNK
    n_chunks = T // C
    s_idx = pl.program_id(1)

    @pl.when(s_idx == 0)
    def _():
        st_ref[...] = jnp.zeros_like(st_ref)

    x = x_ref[0]
    xn = _rms(x, g_ref[layer:layer + 1, :]).astype(BF16)

    lg = lbl_ref[...]
    le = jnp.exp(lg - jnp.max(lg, axis=0, keepdims=True))
    lb = jnp.sum(le[0:layer + 1], axis=0, keepdims=True) / jnp.sum(le, axis=0, keepdims=True)

    pq = _dot(xn, win_ref[:, 0:D])
    q_s[...] = _silu_tanh(pq) * (HG_DK ** -0.5)
    pf = _dot(xn, win_ref[:, D:2 * D])
    forget = lb + (1.0 - lb) * (0.5 + 0.5 * jnp.tanh(0.5 * pf))
    k_s[...] = 1.0 - forget
    logf = jnp.log(forget)

    row = lax.broadcasted_iota(jnp.int32, (C, C), 0)
    col = lax.broadcasted_iota(jnp.int32, (C, C), 1)
    causal = row >= col
    tri = jnp.where(causal, 1.0, 0.0).astype(BF16)
    l_hi = logf.astype(BF16)
    l_lo = (logf - l_hi.astype(F32)).astype(BF16)
    tri2 = jnp.concatenate([tri, tri], axis=1)
    for c in range(n_chunks):
        r = slice(c * C, (c + 1) * C)
        cum_s[r, :] = _dot(tri2, jnp.concatenate([l_hi[r], l_lo[r]], axis=0))

    lasts = jnp.concatenate([cum_s[(c + 1) * C - 1:(c + 1) * C, :] for c in range(n_chunks)], axis=0)
    safe = jnp.min(lasts) > SAFE_LOG_DECAY

    v_groups = 2 if n_chunks % 2 == 0 else 1
    cpg = n_chunks // v_groups

    def stage(fast):
        for c in range(n_chunks):
            r = slice(c * C, (c + 1) * C)
            if fast and c % cpg == 0:
                rows = slice(c * C, (c + cpg) * C)
                v_s[rows, :] = _dot(xn[rows], win_ref[:, 2 * D:3 * D])
            for h in range(HG_HEADS):
                hs = slice(h * HG_DK, (h + 1) * HG_DK)
                cum = cum_s[r, hs]
                last = cum[C - 1:C, :]
                qt = (q_s[r, hs] * jnp.exp(cum)).astype(BF16)
                vt = v_s[r, hs].T.astype(BF16)
                qt_s[r, hs] = qt
                vt_s[c * HG_HEADS + h] = vt
                if fast:
                    kt = k_s[r, hs] * jnp.exp(-cum)
                    a_s[r, hs] = jnp.where(causal, _dot_nt(qt, kt.astype(BF16)), 0.0).astype(BF16)
                    kd = (kt * jnp.exp(last)).astype(BF16)
                else:
                    kd = (k_s[r, hs] * jnp.exp(last - cum)).astype(BF16)
                inc_s[c * HG_HEADS + h] = _dot(vt, kd)

    def scan(fast):
        states = [st_ref[h] for h in range(HG_HEADS)]
        for c in range(n_chunks):
            r = slice(c * C, (c + 1) * C)
            for h in range(HG_HEADS):
                hs = slice(h * HG_DK, (h + 1) * HG_DK)
                st_b = states[h].astype(BF16)
                if fast:
                    o_s[r, hs] = _dot_nt(jnp.concatenate([a_s[r, hs], qt_s[r, hs]], axis=1),
                                         jnp.concatenate([vt_s[c * HG_HEADS + h], st_b], axis=1))
                else:
                    o_s[r, hs] = oi_s[r, hs] + _dot_nt(qt_s[r, hs], st_b)
                decay = jnp.exp(cum_s[(c + 1) * C - 1:(c + 1) * C, hs])
                states[h] = states[h] * decay + inc_s[c * HG_HEADS + h]
        for h in range(HG_HEADS):
            stn_ref[h] = states[h]

    def finish():
        onorm = onorm_ref[layer:layer + 1, :]
        gate = _silu_tanh(_dot(xn, win_ref[:, 3 * D:4 * D]))
        parts = []
        for h in range(HG_HEADS):
            hs = slice(h * HG_DK, (h + 1) * HG_DK)
            parts.append(_rms(o_s[:, hs], onorm) * gate[:, hs])
        o = jnp.concatenate(parts, axis=-1).astype(BF16)
        out_ref[0] = x + _dot(o, wout_ref[...])

    stage(True)
    scan(True)
    finish()

    @pl.when(jnp.logical_not(safe))
    def _():
        tcol = lax.broadcasted_iota(jnp.int32, (C, 1), 0)
        for c in range(n_chunks):
            r = slice(c * C, (c + 1) * C)
            qc = q_s[r, :]
            cumc = cum_s[r, :]

            def body(j, acc):
                kj = k_s[pl.ds(c * C + j, 1), :]
                vj = v_s[pl.ds(c * C + j, 1), :]
                cj = cum_s[pl.ds(c * C + j, 1), :]
                w = qc * kj * jnp.exp(jnp.minimum(cumc - cj, 0.0))
                keep = tcol >= j
                parts = []
                for h in range(HG_HEADS):
                    hs = slice(h * HG_DK, (h + 1) * HG_DK)
                    sc = jnp.sum(w[:, hs], axis=-1, keepdims=True)
                    parts.append(jnp.where(keep, sc, 0.0) * vj[:, hs])
                return acc + jnp.concatenate(parts, axis=-1)

            oi_s[r, :] = lax.fori_loop(0, C, body, jnp.zeros((C, D), F32))
        stage(False)
        scan(False)
        finish()

    st_ref[...] = stn_ref[...]


def _hgrn_layer(x, norm_g, lb_logits, w_in, out_norm, w_out, layer, cast_a, cast_b, cast_layer):
    B, S, D = x.shape
    T = min(HG_TILE, S)
    assert S % T == 0 and T % HG_CHUNK == 0 and D == HG_HEADS * HG_DK
    tile = pl.BlockSpec((1, T, D), lambda b, s: (b, s, 0))
    casts = [_side_cast_specs(w, cast_layer, S // T, B * (S // T)) for w in (cast_a, cast_b)]
    return pl.pallas_call(
        functools.partial(_hgrn_kernel, layer=layer),
        grid=(B, S // T),
        in_specs=[tile, _resident(norm_g.shape), _resident(lb_logits.shape), _resident_layer(w_in.shape, layer),
                  _resident(out_norm.shape), _resident_layer(w_out.shape, layer), casts[0][0], casts[1][0]],
        out_specs=(tile, casts[0][1], casts[1][1]),
        out_shape=(jax.ShapeDtypeStruct(x.shape, F32), casts[0][2], casts[1][2]),
        scratch_shapes=([pltpu.VMEM((HG_HEADS, HG_DK, HG_DK), F32)] * 2 + [pltpu.VMEM((T, D), F32)] * 5
                        + [pltpu.VMEM((T, D), BF16)] * 2
                        + [pltpu.VMEM((T // HG_CHUNK * HG_HEADS, HG_DK, HG_CHUNK), BF16),
                           pltpu.VMEM((T // HG_CHUNK * HG_HEADS, HG_DK, HG_DK), F32)]),
        compiler_params=pltpu.CompilerParams(dimension_semantics=("arbitrary", "arbitrary"),
                                             vmem_limit_bytes=VMEM_LIMIT),
        name="hgrn2_layer",
    )(x, norm_g, lb_logits, w_in, out_norm, w_out, cast_a, cast_b)


def _ffn_kernel(x_ref, g_ref, wup_ref, cw_ref, cb_ref, wdn_ref, fg_ref, out_ref, carry_ref, act_ref,
                *, layer, final_norm):
    T, F = act_ref.shape
    FC = FFN_FCHUNK
    s_idx = pl.program_id(1)
    x = x_ref[0]
    xn = _rms(x, g_ref[layer:layer + 1, :]).astype(BF16)
    has_prev = s_idx > 0
    rid = lax.broadcasted_iota(jnp.int32, (SUBLANES, 1), 0)
    for j in range(F // FC):
        cs = slice(j * FC, (j + 1) * FC)
        gate = _dot(xn, wup_ref[:, cs])
        val = _dot(xn, wup_ref[:, F + j * FC:F + (j + 1) * FC])
        prev = jnp.where(has_prev, carry_ref[:, cs], 0.0)
        carry_ref[:, cs] = gate[T - SUBLANES:, :]
        p1 = prev[SUBLANES - 1:SUBLANES, :]
        p2 = prev[SUBLANES - 2:SUBLANES - 1, :]
        g1 = pltpu.roll(gate, 1, 0)
        g2 = pltpu.roll(gate, 2, 0)
        g1 = jnp.concatenate([jnp.where(rid == 0, p1, g1[:SUBLANES]), g1[SUBLANES:]], axis=0)
        g2 = jnp.concatenate([jnp.where(rid == 0, p2, jnp.where(rid == 1, p1, g2[:SUBLANES])), g2[SUBLANES:]],
                             axis=0)
        conv = cb_ref[layer:layer + 1, cs] + cw_ref[0:1, cs] * g2 + cw_ref[1:2, cs] * g1 + cw_ref[2:3, cs] * gate
        act_ref[:, cs] = (conv * _sigmoid(conv) * val).astype(BF16)
    h = x + _dot(act_ref[...], wdn_ref[...])
    if final_norm:
        h = _rms(h, fg_ref[...])
    out_ref[0] = h


def _ffn_layer(x, norm_g, w_up, conv_w, conv_b, w_down, final_g, layer, final_norm):
    B, S, D = x.shape
    F = w_down.shape[0]
    T = min(FFN_TILE, S)
    assert S % T == 0 and F % FFN_FCHUNK == 0 and conv_w.shape[1] == CONV_WIDTH
    tile = pl.BlockSpec((1, T, D), lambda b, s: (b, s, 0))
    return pl.pallas_call(
        functools.partial(_ffn_kernel, layer=layer, final_norm=final_norm),
        grid=(B, S // T),
        in_specs=[tile, _resident(norm_g.shape), _resident(w_up.shape),
                  _resident_layer(conv_w.shape, layer), _resident(conv_b.shape),
                  _resident(w_down.shape), _resident((1, D))],
        out_specs=tile,
        out_shape=jax.ShapeDtypeStruct(x.shape, F32),
        scratch_shapes=[pltpu.VMEM((SUBLANES, F), F32), pltpu.VMEM((T, F), BF16)],
        compiler_params=pltpu.CompilerParams(dimension_semantics=("arbitrary", "arbitrary"),
                                             vmem_limit_bytes=VMEM_LIMIT),
        name="conv_ffn_final" if final_norm else "conv_ffn",
    )(x, norm_g, w_up, conv_w, conv_b, w_down, final_g.reshape(1, D))


def _attn_kernel(sink_ref, h_ref, ag_ref, kg_ref, wq_ref, wkv_ref, wo_ref, cast_a_ref, cast_b_ref,
                 out_ref, cast_a_out, cast_b_out,
                 kprev_ref, vprev_ref, bias_s, sinkb_s, attn_s, *, layer):
    T, D = attn_s.shape
    cast_a_out[...] = cast_a_ref[...].astype(BF16)
    cast_b_out[...] = cast_b_ref[...].astype(BF16)
    W = WINDOW
    HD = ATT_HEAD_DIM
    KVD = ATT_KV_HEADS * HD
    pairs = ATT_GROUP // 2
    s_idx = pl.program_id(1)

    @pl.when(s_idx == 0)
    def _():
        kprev_ref[...] = jnp.zeros_like(kprev_ref)
        vprev_ref[...] = jnp.zeros_like(vprev_ref)

    h = h_ref[0]
    hn = h * lax.rsqrt(jnp.mean(h * h, axis=-1, keepdims=True) + EPS)
    q = (_dot((hn * ag_ref[layer:layer + 1, :]).astype(BF16), wq_ref[...]) * (HD ** -0.5)).astype(BF16)
    kv = _dot((hn * kg_ref[...]).astype(BF16), wkv_ref[...])
    k_all = jnp.concatenate([kprev_ref[...], kv[:, :KVD]], axis=0)
    v_all = jnp.concatenate([vprev_ref[...], kv[:, KVD:]], axis=0)
    kprev_ref[...] = kv[T - W:, :KVD]
    vprev_ref[...] = kv[T - W:, KVD:]

    R = pairs * W
    ii = lax.broadcasted_iota(jnp.int32, (R, W), 0) & (W - 1)
    jj = lax.broadcasted_iota(jnp.int32, (R, W), 1)
    upper = jj > ii
    dist = jnp.where(upper, ii - jj + W, ii - jj).astype(F32)
    piece = lax.broadcasted_iota(jnp.int32, (R, W), 0) // W
    for kvh in range(ATT_KV_HEADS):
        for parity in range(2):
            slope = jnp.zeros((R, W), F32)
            sink = jnp.zeros((R, W), F32)
            for m in range(pairs):
                hq = kvh * ATT_GROUP + 2 * m + parity
                slope = jnp.where(piece == m, 2.0 ** (-8.0 * (hq + 1) / ATT_Q_HEADS), slope)
                sink = jnp.where(piece == m, sink_ref[layer, hq], sink)
            bias_s[2 * kvh + parity] = -slope * dist
            sinkb_s[2 * kvh + parity] = sink
    first_mask = jnp.where(upper & (s_idx == 0), -jnp.inf, 0.0)

    lane = lax.broadcasted_iota(jnp.int32, (1, KVD), 1)
    low = lane < HD
    high = jnp.logical_not(low)
    k_rot = pltpu.roll(k_all, HD, 1)
    v_rot = pltpu.roll(v_all, HD, 1)

    def padded(t_all, t_rot, kvh, parity):
        src = t_all if kvh == parity else t_rot
        return jnp.where(low if parity == 0 else high, src, 0.0).astype(BF16)

    ones = [jnp.broadcast_to(jnp.where(sel, 1.0, 0.0), (2 * W, KVD)).astype(BF16) for sel in (low, high)]

    for kvh in range(ATT_KV_HEADS):
        k_pad = [padded(k_all, k_rot, kvh, parity) for parity in range(2)]
        v_pad = [padded(v_all, v_rot, kvh, parity) for parity in range(2)]
        for n in range(T // W):
            rows = slice(n * W, (n + 1) * W)
            band = slice(n * W, (n + 2) * W)
            qs = jnp.concatenate(
                [q[rows, (kvh * pairs + m) * LANES:(kvh * pairs + m + 1) * LANES] for m in range(pairs)],
                axis=0)
            probs, sink_terms = [], []
            for parity in range(2):
                sc = _dot_nt(qs, k_pad[parity][band])
                f = jnp.where(upper, sc[:, :W], sc[:, W:]) + bias_s[2 * kvh + parity]
                if n == 0:
                    f = f + first_mask
                mx = jnp.broadcast_to(jnp.max(f, axis=-1, keepdims=True), (R, W))
                e = jnp.exp(f - mx)
                sink_terms.append(jnp.exp(sinkb_s[2 * kvh + parity] - mx))
                probs.append(jnp.concatenate([jnp.where(upper, e, 0.0), jnp.where(upper, 0.0, e)],
                                             axis=1).astype(BF16))
            lhs = jnp.concatenate(probs, axis=1)
            rhs = jnp.concatenate(
                [jnp.concatenate([v_pad[parity][band], ones[parity]], axis=1) for parity in range(2)],
                axis=0)
            pv = _dot(lhs, rhs)
            out = pv[:, :KVD] / (pv[:, KVD:] + jnp.where(low, sink_terms[0], sink_terms[1]))
            for m in range(pairs):
                cols = slice((kvh * pairs + m) * LANES, (kvh * pairs + m + 1) * LANES)
                attn_s[rows, cols] = out[m * W:(m + 1) * W]

    out_ref[0] = h + _dot(attn_s[...].astype(BF16), wo_ref[...])


def _attn_layer(h, attn_g, kv_g, w_q, w_kv, sinks, w_o, layer, cast_a, cast_b, cast_layer):
    B, S, D = h.shape
    T = min(ATT_TILE, S)
    assert S % T == 0 and T % WINDOW == 0
    assert w_q.shape[2] == ATT_Q_HEADS * ATT_HEAD_DIM and w_kv.shape[1] == 2 * ATT_KV_HEADS * ATT_HEAD_DIM
    tile = pl.BlockSpec((1, T, D), lambda b, s: (b, s, 0))
    kvd = ATT_KV_HEADS * ATT_HEAD_DIM
    casts = [_side_cast_specs(w, cast_layer, S // T, B * (S // T)) for w in (cast_a, cast_b)]
    return pl.pallas_call(
        functools.partial(_attn_kernel, layer=layer),
        grid=(B, S // T),
        in_specs=[pl.BlockSpec(memory_space=pltpu.SMEM), tile, _resident(attn_g.shape), _resident((1, D)),
                  _resident_layer(w_q.shape, layer), _resident(w_kv.shape), _resident_layer(w_o.shape, layer),
                  casts[0][0], casts[1][0]],
        out_specs=(tile, casts[0][1], casts[1][1]),
        out_shape=(jax.ShapeDtypeStruct(h.shape, F32), casts[0][2], casts[1][2]),
        scratch_shapes=[pltpu.VMEM((WINDOW, kvd), F32), pltpu.VMEM((WINDOW, kvd), F32),
                        pltpu.VMEM((2 * ATT_KV_HEADS, ATT_GROUP // 2 * WINDOW, WINDOW), F32),
                        pltpu.VMEM((2 * ATT_KV_HEADS, ATT_GROUP // 2 * WINDOW, WINDOW), F32),
                        pltpu.VMEM((T, ATT_Q_HEADS * ATT_HEAD_DIM), F32)],
        compiler_params=pltpu.CompilerParams(dimension_semantics=("arbitrary", "arbitrary"),
                                             vmem_limit_bytes=VMEM_LIMIT),
        name="swa_layer",
    )(sinks, h, attn_g, kv_g.reshape(1, D), w_q, w_kv, w_o, cast_a, cast_b)


def kernel(x, hg_norm, hg_w_in, hg_lb_logits, hg_out_norm, hg_w_out, kv_norm, w_kv, attn_norm, attn_w_q,
           attn_sinks, attn_w_o, ffn_norm, ffn_w_up, ffn_conv_w, ffn_conv_b, ffn_w_down, final_norm):
    depth = ffn_norm.shape[0]
    n_a = hg_norm.shape[0]
    assert depth - n_a == 1
    h = x
    for layer in range(depth):
        if layer < n_a:
            h, w_up, w_down = _hgrn_layer(h, hg_norm, hg_lb_logits, hg_w_in, hg_out_norm, hg_w_out, layer,
                                          ffn_w_up, ffn_w_down, layer)
        else:
            h, w_up, w_down = _attn_layer(h, attn_norm, kv_norm, attn_w_q, w_kv, attn_sinks, attn_w_o,
                                          layer - n_a, ffn_w_up, ffn_w_down, layer)
        h = _ffn_layer(h, ffn_norm, w_up, ffn_conv_w, ffn_conv_b, w_down, final_norm, layer,
                       layer == depth - 1)
    return h
```
